```python
import math
import jax
import jax.numpy as jnp
from jax import lax
import numpy as np

D_MODEL = 1024
BATCH = 4
SEQ = 4096
DEPTH = 4
DEC_BATCH = 32
DEC_SEQ = 4
PAST_LEN = 8192
PAGE_SIZE = 128

HEAD_DIM = 64
MIX_WIDTH = D_MODEL
ATT_HEADS = (MIX_WIDTH // 2) // HEAD_DIM
ATT_WIDTH = ATT_HEADS * HEAD_DIM
RW_HEADS = (MIX_WIDTH - ATT_WIDTH) // HEAD_DIM
RW_WIDTH = RW_HEADS * HEAD_DIM
MOBA_BLOCK = 256
MOBA_TOPK = 3
Q_CHUNK = 32
ROT_DIM = HEAD_DIM // 4
ROPE_THETA = 500000.0
LORA_DECAY = 64
LORA_A = 64
LORA_GATE = 128
RW_COLS = 3 * RW_WIDTH + LORA_DECAY + LORA_A + LORA_GATE
IN_COLS = 3 * ATT_WIDTH + RW_COLS
FFN_HIDDEN = -(-(8 * D_MODEL) // (3 * 256)) * 256
RMS_EPS = 1e-6
GN_EPS = 64e-5
NEG_INF = -1e30

kernel_name = "moba_rwkv7_hybrid_step"


def rms_norm(x, g):
    xf = x.astype(jnp.float32)
    y = xf * lax.rsqrt(jnp.mean(xf * xf, axis=-1, keepdims=True) + RMS_EPS)
    return (y * g.astype(jnp.float32)).astype(x.dtype)


def partial_rope(x, pos):
    half = ROT_DIM // 2
    inv = jnp.float32(ROPE_THETA) ** (-jnp.arange(half, dtype=jnp.float32) / half)
    ang = pos.astype(jnp.float32)[:, None] * inv[None, :]
    cos = jnp.cos(ang)[None, :, None, :]
    sin = jnp.sin(ang)[None, :, None, :]
    xf = x.astype(jnp.float32)
    x1 = xf[..., :half]
    x2 = xf[..., half:ROT_DIM]
    out = jnp.concatenate([x1 * cos - x2 * sin, x2 * cos + x1 * sin, xf[..., ROT_DIM:]], axis=-1)
    return out.astype(x.dtype)


def moba_attention(q, k, v, q_pos):
    B, T, H, D = q.shape
    Tk = k.shape[1]
    nb = -(-Tk // MOBA_BLOCK)
    pad = nb * MOBA_BLOCK - Tk
    kp = jnp.pad(k, ((0, 0), (0, pad), (0, 0), (0, 0)))
    vp = jnp.pad(v, ((0, 0), (0, pad), (0, 0), (0, 0)))
    kbh = kp.reshape(B, nb, MOBA_BLOCK, H, D).transpose(0, 3, 1, 2, 4)
    vbh = vp.reshape(B, nb, MOBA_BLOCK, H, D).transpose(0, 3, 1, 2, 4)
    kmean = jnp.mean(kbh.astype(jnp.float32), axis=3)
    topk = min(MOBA_TOPK, nb)
    qc = math.gcd(T, Q_CHUNK)
    nc = T // qc
    qh = q.transpose(0, 2, 1, 3).reshape(B, H, nc, qc, D).transpose(2, 0, 1, 3, 4)
    posc = q_pos.reshape(nc, qc)
    bi = jnp.arange(B)[:, None, None, None]
    hi = jnp.arange(H)[None, :, None, None]
    blk_ids = jnp.arange(nb)
    key_off = jnp.arange(MOBA_BLOCK)
    scale = HEAD_DIM ** -0.5

    def one_chunk(args):
        qs, pos = args
        own = pos // MOBA_BLOCK
        gate = jnp.einsum('bhqd,bhnd->bhqn', qs.astype(jnp.float32), kmean)
        gate = jnp.where(blk_ids[None, None, None, :] < own[None, None, :, None], gate, NEG_INF)
        _, sel = lax.top_k(gate, topk)
        valid = sel < own[None, None, :, None]
        k_sel = kbh[bi, hi, sel]
        v_sel = vbh[bi, hi, sel]
        k_own = kbh[bi[..., 0], hi[..., 0], own[None, None, :]]
        v_own = vbh[bi[..., 0], hi[..., 0], own[None, None, :]]
        s_sel = jnp.einsum('bhqd,bhqkjd->bhqkj', qs, k_sel).astype(jnp.float32) * scale
        s_sel = jnp.where(valid[..., None], s_sel, NEG_INF).reshape(B, H, qc, topk * MOBA_BLOCK)
        s_own = jnp.einsum('bhqd,bhqjd->bhqj', qs, k_own).astype(jnp.float32) * scale
        own_pos = own[:, None] * MOBA_BLOCK + key_off[None, :]
        s_own = jnp.where((own_pos <= pos[:, None])[None, None], s_own, NEG_INF)
        p = jax.nn.softmax(jnp.concatenate([s_sel, s_own], axis=-1), axis=-1).astype(v.dtype)
        p_sel = p[..., :topk * MOBA_BLOCK].reshape(B, H, qc, topk, MOBA_BLOCK)
        p_own = p[..., topk * MOBA_BLOCK:]
        return (jnp.einsum('bhqkj,bhqkjd->bhqd', p_sel, v_sel)
                + jnp.einsum('bhqj,bhqjd->bhqd', p_own, v_own))

    out = lax.map(one_chunk, (qh, posc))
    return out.transpose(1, 0, 3, 2, 4).reshape(B, T, H * D)


def wkv_scan(r, decay, kk, b, k, v, S0):
    def step(S, inp):
        r_t, w_t, kk_t, b_t, k_t, v_t = inp
        sa = jnp.einsum('bhvk,bhk->bhv', S, kk_t)
        S = S * w_t[:, :, None, :] - sa[..., None] * b_t[:, :, None, :] + v_t[..., None] * k_t[:, :, None, :]
        y = jnp.einsum('bhvk,bhk->bhv', S, r_t)
        return S, y
    xs = tuple(jnp.moveaxis(t, 1, 0) for t in (r, decay, kk, b, k, v))
    S, ys = lax.scan(step, S0, xs)
    return S, jnp.moveaxis(ys, 0, 1)


def rwkv7_time_mix(rw, shift0, wkv0, shift_mu, decay_w0, decay_up, iclr_a0, iclr_up, gate_up,
                   k_k, k_a, r_k, lnx_g, lnx_b):
    B, T, _ = rw.shape
    rwf = rw.astype(jnp.float32)
    prev = jnp.concatenate([shift0.astype(jnp.float32)[:, None, :], rwf[:, :-1]], axis=1)
    xs = rwf + shift_mu.astype(jnp.float32) * (prev - rwf)
    r = xs[..., :RW_WIDTH]
    kr = xs[..., RW_WIDTH:2 * RW_WIDTH]
    vr = xs[..., 2 * RW_WIDTH:3 * RW_WIDTH]
    off = 3 * RW_WIDTH
    wd = xs[..., off:off + LORA_DECAY]
    ad = xs[..., off + LORA_DECAY:off + LORA_DECAY + LORA_A]
    gd = xs[..., off + LORA_DECAY + LORA_A:]
    w = decay_w0 + jnp.tanh(wd) @ decay_up
    decay = jnp.exp(-jnp.exp(-jax.nn.softplus(-w) - 0.5))
    a = jax.nn.sigmoid(iclr_a0 + ad @ iclr_up)
    g = jax.nn.sigmoid(gd) @ gate_up
    kk = kr * k_k
    k_mod = kr * (1.0 + (a - 1.0) * k_a)

    def heads(t):
        return t.reshape(B, T, RW_HEADS, HEAD_DIM).astype(jnp.float32)

    r_h, k_h, v_h, w_h, a_h, kk_h = heads(r), heads(k_mod), heads(vr), heads(decay), heads(a), heads(kk)
    kk_h = kk_h / jnp.maximum(jnp.sqrt(jnp.sum(kk_h * kk_h, axis=-1, keepdims=True)), 1e-12)
    b_h = kk_h * a_h
    S, y = wkv_scan(r_h, w_h, kk_h, b_h, k_h, v_h, wkv0.astype(jnp.float32))
    mean = jnp.mean(y, axis=-1, keepdims=True)
    var = jnp.mean((y - mean) ** 2, axis=-1, keepdims=True)
    yn = ((y - mean) * lax.rsqrt(var + GN_EPS)).reshape(B, T, RW_WIDTH) * lnx_g + lnx_b
    bonus = jnp.sum(r_h * k_h * r_k.astype(jnp.float32), axis=-1, keepdims=True) * v_h
    out = (yn + bonus.reshape(B, T, RW_WIDTH)) * g
    return out.astype(rw.dtype), S, rw[:, -1]


def decoder_layer(x, pos, past_k, past_v, wkv0, shift0, norm_attn_g, w_in, shift_mu, decay_w0, decay_up,
                  iclr_a0, iclr_up, gate_up, k_k, k_a, r_k, lnx_g, lnx_b, w_out, norm_ffn_g,
                  w_ffn_in, w_ffn_out):
    B, T, _ = x.shape
    h = rms_norm(x, norm_attn_g)
    proj = h @ w_in
    q = proj[..., :ATT_WIDTH].reshape(B, T, ATT_HEADS, HEAD_DIM)
    k = proj[..., ATT_WIDTH:2 * ATT_WIDTH].reshape(B, T, ATT_HEADS, HEAD_DIM)
    v = proj[..., 2 * ATT_WIDTH:3 * ATT_WIDTH].reshape(B, T, ATT_HEADS, HEAD_DIM)
    rw = proj[..., 3 * ATT_WIDTH:]
    q = partial_rope(q, pos)
    k = partial_rope(k, pos)
    if past_k is None:
        k_all, v_all = k, v
    else:
        k_all = jnp.concatenate([past_k.astype(k.dtype), k], axis=1)
        v_all = jnp.concatenate([past_v.astype(v.dtype), v], axis=1)
    attn = moba_attention(q, k_all, v_all, pos)
    rwkv, wkv_new, shift_new = rwkv7_time_mix(rw, shift0, wkv0, shift_mu, decay_w0, decay_up, iclr_a0,
                                               iclr_up, gate_up, k_k, k_a, r_k, lnx_g, lnx_b)
    x = x + jnp.concatenate([attn, rwkv], axis=-1) @ w_out
    h2 = rms_norm(x, norm_ffn_g)
    gu = h2 @ w_ffn_in
    x = x + (jax.nn.silu(gu[..., :FFN_HIDDEN]) * gu[..., FFN_HIDDEN:]) @ w_ffn_out
    return x, k, v, wkv_new, shift_new


def setup_inputs(seed: int = 0) -> dict:
    key = jax.random.key(seed)
    ks = jax.random.split(key, 32)
    f32 = jnp.float32
    n_pages = PAST_LEN // PAGE_SIZE
    used = DEC_BATCH * n_pages
    n_pool = used + max(1, used // 4)
    nrm = lambda k, shape, s: jax.random.normal(k, shape, f32) * s
    page_table = jax.random.permutation(ks[0], n_pool)[:used].reshape(DEC_BATCH, n_pages).astype(jnp.int32)
    return {
        "x_prompt": nrm(ks[1], (BATCH, SEQ, D_MODEL), 1.0),
        "x_sample": nrm(ks[2], (DEC_BATCH, DEC_SEQ, D_MODEL), 1.0),
        "cache_k": nrm(ks[3], (DEPTH, n_pool, PAGE_SIZE, ATT_HEADS, HEAD_DIM), 1.0),
        "cache_v": nrm(ks[4], (DEPTH, n_pool, PAGE_SIZE, ATT_HEADS, HEAD_DIM), 1.0),
        "state_wkv": nrm(ks[5], (DEPTH, DEC_BATCH, RW_HEADS, HEAD_DIM, HEAD_DIM), 0.1),
        "state_shift": nrm(ks[6], (DEPTH, DEC_BATCH, RW_COLS), 1.0),
        "page_table": page_table,
        "norm_attn_g": 1.0 + nrm(ks[7], (DEPTH, D_MODEL), 0.05),
        "w_in": nrm(ks[8], (DEPTH, D_MODEL, IN_COLS), D_MODEL ** -0.5),
        "shift_mu": jax.random.uniform(ks[9], (DEPTH, RW_COLS), f32),
        "decay_w0": jax.random.uniform(ks[10], (DEPTH, RW_WIDTH), f32, -6.0, 0.0),
        "decay_up": nrm(ks[11], (DEPTH, LORA_DECAY, RW_WIDTH), 0.5 * LORA_DECAY ** -0.5),
        "iclr_a0": nrm(ks[12], (DEPTH, RW_WIDTH), 0.5),
        "iclr_up": nrm(ks[13], (DEPTH, LORA_A, RW_WIDTH), 0.5 * LORA_A ** -0.5),
        "gate_up": nrm(ks[14], (DEPTH, LORA_GATE, RW_WIDTH), LORA_GATE ** -0.5),
        "k_k": 0.85 + nrm(ks[15], (DEPTH, RW_WIDTH), 0.05),
        "k_a": 1.0 + nrm(ks[16], (DEPTH, RW_WIDTH), 0.05),
        "r_k": nrm(ks[17], (DEPTH, RW_HEADS, HEAD_DIM), 0.1),
        "lnx_g": 1.0 + nrm(ks[18], (DEPTH, RW_WIDTH), 0.05),
        "lnx_b": nrm(ks[19], (DEPTH, RW_WIDTH), 0.02),
        "w_out": nrm(ks[20], (DEPTH, MIX_WIDTH, D_MODEL), MIX_WIDTH ** -0.5),
        "norm_ffn_g": 1.0 + nrm(ks[21], (DEPTH, D_MODEL), 0.05),
        "w_ffn_in": nrm(ks[22], (DEPTH, D_MODEL, 2 * FFN_HIDDEN), D_MODEL ** -0.5),
        "w_ffn_out": nrm(ks[23], (DEPTH, FFN_HIDDEN, D_MODEL), FFN_HIDDEN ** -0.5),
        "norm_final_g": 1.0 + nrm(ks[24], (D_MODEL,), 0.05),
    }


def reference(x_prompt, x_sample, cache_k, cache_v, state_wkv, state_shift, page_table, norm_attn_g, w_in,
              shift_mu, decay_w0, decay_up, iclr_a0, iclr_up, gate_up, k_k, k_a, r_k, lnx_g, lnx_b, w_out,
              norm_ffn_g, w_ffn_in, w_ffn_out, norm_final_g):
    B, T, _ = x_prompt.shape
    DB, TS, _ = x_sample.shape
    n_pages = page_table.shape[1]
    past_len = n_pages * cache_k.shape[2]
    pos_p = jnp.arange(T, dtype=jnp.int32)
    pos_s = past_len + jnp.arange(TS, dtype=jnp.int32)
    wkv_zero = jnp.zeros((B, RW_HEADS, HEAD_DIM, HEAD_DIM), jnp.float32)
    shift_zero = jnp.zeros((B, RW_COLS), x_prompt.dtype)
    xp, xs = x_prompt, x_sample
    kp_l, vp_l, sp_l, hp_l = [], [], [], []
    ks_l, vs_l, ss_l, hs_l = [], [], [], []
    for l in range(DEPTH):
        lw = (norm_attn_g[l], w_in[l], shift_mu[l], decay_w0[l], decay_up[l], iclr_a0[l], iclr_up[l],
              gate_up[l], k_k[l], k_a[l], r_k[l], lnx_g[l], lnx_b[l], w_out[l], norm_ffn_g[l],
              w_ffn_in[l], w_ffn_out[l])
        xp, kp, vp, sp, hp = decoder_layer(xp, pos_p, None, None, wkv_zero, shift_zero, *lw)
        past_k = cache_k[l][page_table].reshape(DB, past_len, ATT_HEADS, HEAD_DIM)
        past_v = cache_v[l][page_table].reshape(DB, past_len, ATT_HEADS, HEAD_DIM)
        xs, kn, vn, sn, hn = decoder_layer(xs, pos_s, past_k, past_v, state_wkv[l], state_shift[l], *lw)
        kp_l.append(kp); vp_l.append(vp); sp_l.append(sp); hp_l.append(hp)
        ks_l.append(kn); vs_l.append(vn); ss_l.append(sn); hs_l.append(hn)
    y_prompt = rms_norm(xp, norm_final_g)
    y_sample = rms_norm(xs, norm_final_g)
    k_prompt = jnp.stack(kp_l)
    v_prompt = jnp.stack(vp_l)
    wkv_prompt = jnp.stack(sp_l)
    shift_prompt = jnp.stack(hp_l)
    k_sample = jnp.stack(ks_l)
    v_sample = jnp.stack(vs_l)
    wkv_sample = jnp.stack(ss_l)
    shift_sample = jnp.stack(hs_l)
    return (y_prompt, y_sample, k_prompt, v_prompt, wkv_prompt, shift_prompt, k_sample, v_sample, wkv_sample, shift_sample)
```

```python
import functools

import jax
import jax.numpy as jnp
from jax import lax
from jax.experimental import pallas as pl
from jax.experimental.pallas import tpu as pltpu

F32 = jnp.float32
BF16 = jnp.bfloat16

HEAD_DIM = 64
MOBA_BLOCK = 256
MOBA_TOPK = 3
ROT_DIM = HEAD_DIM // 4
ROPE_THETA = 500000.0
LORA_DECAY = 64
LORA_A = 64
LORA_GATE = 128
RMS_EPS = 1e-6
GN_EPS = 64e-5
NEG_INF = -1e30

QUAD_HEADS = 4
QUAD = QUAD_HEADS * HEAD_DIM
WKV_CHUNK = 64
WKV_SPLIT = 2
VMEM_LIMIT = 48 * 1024 * 1024

_NN = (((1,), (0,)), ((), ()))
_NT = (((1,), (1,)), ((), ()))
_TN = (((0,), (0,)), ((), ()))


def _split(x, n):
    parts = []
    for i in range(n):
        p = x.astype(BF16)
        parts.append(p)
        if i + 1 < n:
            x = x - p.astype(F32)
    return parts


def _mdot(pa, pb, dims=_NN):
    order = max(len(pa), len(pb))
    acc = None
    for i in reversed(range(len(pa))):
        for j in reversed(range(len(pb))):
            if i + j < order:
                t = lax.dot_general(pa[i], pb[j], dims, preferred_element_type=F32)
                acc = t if acc is None else acc + t
    return acc


def _dot(a, b, dims=_NN):
    return lax.dot_general(a, b, dims, preferred_element_type=F32)


def _iota(shape, dim):
    return lax.broadcasted_iota(jnp.int32, shape, dim)


def _top3(gate, nvalid):
    blk = _iota(gate.shape, 1)
    g = jnp.where(blk < nvalid, gate, NEG_INF)
    sels = []
    for _ in range(MOBA_TOPK):
        m = jnp.max(g, axis=1, keepdims=True)
        idx = jnp.min(jnp.where(g == m, blk, jnp.int32(1 << 30)), axis=1, keepdims=True)
        sels.append(jnp.where(m > 0.5 * NEG_INF, idx, -1))
        g = jnp.where(blk == idx, NEG_INF, g)
    return sels


def _norm_proj_kernel(x_ref, g_ref, w_ref, cos_ref, sna_ref, snb_ref, q_ref, k_ref, v_ref, rw_ref, *, att):
    x = x_ref[...]
    ms = jnp.mean(x * x, axis=-1, keepdims=True)
    h = (x * lax.rsqrt(ms + RMS_EPS) * g_ref[...]).astype(BF16)
    cos, sna, snb = cos_ref[...], sna_ref[...], snb_ref[...]
    half = ROT_DIM // 2

    def rope(p):
        return p * cos + pltpu.roll(p, att - half, 1) * sna + pltpu.roll(p, half, 1) * snb

    q_ref[...] = rope(_dot(h, w_ref[:, 0:att]))
    k_ref[...] = rope(_dot(h, w_ref[:, att:2 * att]))
    v_ref[...] = _dot(h, w_ref[:, 2 * att:3 * att])
    rw_ref[...] = _dot(h, w_ref[:, 3 * att:])


def _norm_proj(x2d, g, w_b, tabs, tm, att):
    n, d = x2d.shape
    cols = w_b.shape[1]
    rwc = cols - 3 * att
    ntab = tabs[0].shape[0] // tm
    row = lambda i: (i, 0)
    tab = lambda i: (i % ntab, 0)
    const = lambda i: (0, 0)
    return pl.pallas_call(
        functools.partial(_norm_proj_kernel, att=att),
        grid=(n // tm,),
        in_specs=[
            pl.BlockSpec((tm, d), row),
            pl.BlockSpec((1, d), const),
            pl.BlockSpec((d, cols), const),
            pl.BlockSpec((tm, att), tab),
            pl.BlockSpec((tm, att), tab),
            pl.BlockSpec((tm, att), tab),
        ],
        out_specs=[
            pl.BlockSpec((tm, att), row),
            pl.BlockSpec((tm, att), row),
            pl.BlockSpec((tm, att), row),
            pl.BlockSpec((tm, rwc), row),
        ],
        out_shape=[
            jax.ShapeDtypeStruct((n, att), F32),
            jax.ShapeDtypeStruct((n, att), F32),
            jax.ShapeDtypeStruct((n, att), F32),
            jax.ShapeDtypeStruct((n, rwc), F32),
        ],
        compiler_params=pltpu.CompilerParams(
            dimension_semantics=("arbitrary",), vmem_limit_bytes=VMEM_LIMIT),
        name="norm_proj",
    )(x2d, g, w_b, *tabs)


def _rope_tables(pos, att):
    half = ROT_DIM // 2
    inv = jnp.float32(ROPE_THETA) ** (-jnp.arange(half, dtype=F32) / half)
    ang = pos.astype(F32)[:, None] * inv[None, :]
    cos, sin = jnp.cos(ang), jnp.sin(ang)
    t = pos.shape[0]
    rest = HEAD_DIM - ROT_DIM
    cos_h = jnp.concatenate([cos, cos, jnp.ones((t, rest), F32)], axis=1)
    sna_h = jnp.concatenate([-sin, jnp.zeros((t, half + rest), F32)], axis=1)
    snb_h = jnp.concatenate([jnp.zeros((t, half), F32), sin, jnp.zeros((t, rest), F32)], axis=1)
    nh = att // HEAD_DIM
    return tuple(jnp.tile(a, (1, nh)) for a in (cos_h, sna_h, snb_h))


def _moba_prompt_kernel(q_ref, k_ref, v_ref, o_ref, kmean_ref, m_ref, l_ref, acc_ref, *, nb):
    i = pl.program_id(2)
    blk = MOBA_BLOCK

    @pl.when(i == 0)
    def _():
        for j in range(nb):
            kmean_ref[j:j + 1, :] = jnp.mean(k_ref[0, j * blk:(j + 1) * blk, :], axis=0, keepdims=True)

    lane_head = _iota((1, QUAD), 1) // HEAD_DIM
    qv = q_ref[0] * (HEAD_DIM ** -0.5)
    causal = _iota((blk, blk), 1) <= _iota((blk, blk), 0)
    kmean = _split(kmean_ref[...], 2)
    start = pl.multiple_of(i * blk, blk)
    k_own = k_ref[0, pl.ds(start, blk), :].astype(BF16)
    v_own = v_ref[0, pl.ds(start, blk), :].astype(BF16)

    qms, sels = [], []
    acc = None
    for h in range(QUAD_HEADS):
        qm = jnp.where(lane_head == h, qv, 0.0)
        sels.append(_top3(_mdot(_split(qm, 2), kmean, _NT), i))
        qmb = qm.astype(BF16)
        qms.append(qmb)
        s = jnp.where(causal, _dot(qmb, k_own, _NT), NEG_INF)
        m = jnp.max(s, axis=1, keepdims=True)
        p = jnp.exp(s - m)
        m_ref[h] = m
        l_ref[h] = jnp.sum(p, axis=1, keepdims=True)
        pv = _dot(p.astype(BF16), v_own)
        acc = pv if acc is None else jnp.where(lane_head == h, pv, acc)
    acc_ref[...] = acc

    def body(j, carry):
        off = pl.multiple_of(j * blk, blk)
        kj = k_ref[0, pl.ds(off, blk), :].astype(BF16)
        vj = v_ref[0, pl.ds(off, blk), :].astype(BF16)
        acc = acc_ref[...]
        for h in range(QUAD_HEADS):
            s1, s2, s3 = sels[h]
            picked = (s1 == j) | (s2 == j) | (s3 == j)
            s = jnp.where(picked, _dot(qms[h], kj, _NT), NEG_INF)
            m_old = m_ref[h]
            m_new = jnp.maximum(m_old, jnp.max(s, axis=1, keepdims=True))
            alpha = jnp.exp(m_old - m_new)
            p = jnp.exp(s - m_new)
            l_ref[h] = alpha * l_ref[h] + jnp.sum(p, axis=1, keepdims=True)
            m_ref[h] = m_new
            pv = _dot(p.astype(BF16), vj)
            acc = jnp.where(lane_head == h, acc * alpha + pv, acc)
        acc_ref[...] = acc
        return carry

    lax.fori_loop(0, i, body, 0)

    out = acc_ref[...]
    for h in range(QUAD_HEADS):
        out = jnp.where(lane_head == h, out / l_ref[h], out)
    o_ref[0] = out


def _moba_prompt(q, k, v):
    b, t, att = q.shape
    nb = t // MOBA_BLOCK
    nq = att // QUAD
    return pl.pallas_call(
        functools.partial(_moba_prompt_kernel, nb=nb),
        grid=(b, nq, nb),
        in_specs=[
            pl.BlockSpec((1, MOBA_BLOCK, QUAD), lambda bi, qi, i: (bi, i, qi)),
            pl.BlockSpec((1, t, QUAD), lambda bi, qi, i: (bi, 0, qi)),
            pl.BlockSpec((1, t, QUAD), lambda bi, qi, i: (bi, 0, qi)),
        ],
        out_specs=pl.BlockSpec((1, MOBA_BLOCK, QUAD), lambda bi, qi, i: (bi, i, qi)),
        out_shape=jax.ShapeDtypeStruct((b, t, att), F32),
        scratch_shapes=[
            pltpu.VMEM((nb, QUAD), F32),
            pltpu.VMEM((QUAD_HEADS, MOBA_BLOCK, 1), F32),
            pltpu.VMEM((QUAD_HEADS, MOBA_BLOCK, 1), F32),
            pltpu.VMEM((MOBA_BLOCK, QUAD), F32),
        ],
        compiler_params=pltpu.CompilerParams(
            dimension_semantics=("arbitrary", "arbitrary", "arbitrary"), vmem_limit_bytes=VMEM_LIMIT),
        name="moba_prompt",
    )(q, k, v)


PAGES_PER_STEP = 8


def _moba_sample_kernel(pt_ref, q_ref, kn_ref, vn_ref, *refs, ts, npages, page, att):
    g_pages = PAGES_PER_STEP
    k_refs = refs[:g_pages]
    v_refs = refs[g_pages:2 * g_pages]
    o_ref = refs[2 * g_pages]
    qbd_ref, kmean_ref, s_ref, sel_ref, m_ref, l_ref, acc_ref = refs[2 * g_pages + 1:]
    ph = pl.program_id(1)
    p = pl.program_id(2)
    nsteps = npages // g_pages
    nh = att // HEAD_DIM
    rows = ts * nh
    pages_per_block = MOBA_BLOCK // page
    nblk = npages // pages_per_block
    row_t = _iota((rows, 1), 0) // nh

    @pl.when((ph == 0) & (p == 0))
    def _():
        qv = q_ref[0] * (HEAD_DIM ** -0.5)
        qrep = jnp.concatenate([jnp.broadcast_to(qv[t:t + 1, :], (nh, att)) for t in range(ts)], axis=0)
        own_head = (_iota((rows, att), 1) // HEAD_DIM) == (_iota((rows, att), 0) % nh)
        qbd_ref[...] = jnp.where(own_head, qrep, 0.0)
        kmean_ref[...] = jnp.zeros_like(kmean_ref)

    @pl.when(ph == 0)
    def _():
        qb = qbd_ref[...].astype(BF16)
        ksum = kmean_ref[...]
        for g in range(g_pages):
            pg = p * g_pages + g
            kp = k_refs[g][0, 0]
            s_ref[pg] = _dot(qb, kp.astype(BF16), _NT)
            colsum = jnp.sum(kp, axis=0, keepdims=True)
            ksum = ksum + jnp.where(_iota((nblk, 1), 0) == pg // pages_per_block, colsum, 0.0)
        kmean_ref[...] = ksum

    @pl.when((ph == 0) & (p == nsteps - 1))
    def _():
        qbd = qbd_ref[...]
        gate = _mdot(_split(qbd, 2), _split(kmean_ref[...] * (1.0 / MOBA_BLOCK), 2), _NT)
        sels = _top3(gate, nblk)
        for r in range(MOBA_TOPK):
            sel_ref[r] = sels[r]

        def mx(pg, m):
            b_ = pg // pages_per_block
            picked = (sels[0] == b_) | (sels[1] == b_) | (sels[2] == b_)
            return jnp.maximum(m, jnp.max(jnp.where(picked, s_ref[pg], NEG_INF), axis=1, keepdims=True))

        m = lax.fori_loop(0, npages, mx, jnp.full((rows, 1), NEG_INF, F32))
        s_own = []
        for t in range(ts):
            st = jnp.sum(qbd * kn_ref[0, t:t + 1, :], axis=1, keepdims=True)
            st = jnp.where(row_t >= t, st, NEG_INF)
            s_own.append(st)
            m = jnp.maximum(m, st)
        l = jnp.zeros((rows, 1), F32)
        acc = jnp.zeros((rows, att), F32)
        for t in range(ts):
            pt_ = jnp.exp(s_own[t] - m)
            l = l + pt_
            acc = acc + pt_ * vn_ref[0, t:t + 1, :]
        m_ref[...] = m
        l_ref[...] = l
        acc_ref[...] = acc

    @pl.when(ph == 1)
    def _():
        m = m_ref[...]
        l = l_ref[...]
        acc = acc_ref[...]
        for g in range(g_pages):
            pg = p * g_pages + g
            b_ = pg // pages_per_block
            picked = (sel_ref[0] == b_) | (sel_ref[1] == b_) | (sel_ref[2] == b_)
            pe = jnp.where(picked, jnp.exp(s_ref[pg] - m), 0.0)
            l = l + jnp.sum(pe, axis=1, keepdims=True)
            acc = acc + _dot(pe.astype(BF16), v_refs[g][0, 0].astype(BF16))
        l_ref[...] = l
        acc_ref[...] = acc

    @pl.when((ph == 1) & (p == nsteps - 1))
    def _():
        full = acc_ref[...] / l_ref[...]
        own_head = (_iota((rows, att), 1) // HEAD_DIM) == (_iota((rows, att), 0) % nh)
        full = jnp.where(own_head, full, 0.0)
        o_ref[0] = jnp.concatenate(
            [jnp.sum(full[t * nh:(t + 1) * nh, :], axis=0, keepdims=True) for t in range(ts)], axis=0)


def _moba_sample(q, k_new, v_new, cache_k, cache_v, page_table, layer):
    db, ts, att = q.shape
    npages = page_table.shape[1]
    page = cache_k.shape[2]
    g_pages = PAGES_PER_STEP
    nsteps = npages // g_pages
    nh = att // HEAD_DIM
    rows = ts * nh
    nblk = npages * page // MOBA_BLOCK

    def k_map(g):
        def f(b, ph, p, pt):
            pg = jnp.where(ph == 0, p, nsteps - 1) * g_pages + g
            return (layer, pt[b, pg], 0, 0)
        return f

    def v_map(g):
        def f(b, ph, p, pt):
            pg = jnp.where(ph == 0, 0, p) * g_pages + g
            return (layer, pt[b, pg], 0, 0)
        return f

    tok = lambda b, ph, p, pt: (b, 0, 0)
    grid_spec = pltpu.PrefetchScalarGridSpec(
        num_scalar_prefetch=1,
        grid=(db, 2, nsteps),
        in_specs=(
            [pl.BlockSpec((1, ts, att), tok)] * 3
            + [pl.BlockSpec((1, 1, page, att), k_map(g)) for g in range(g_pages)]
            + [pl.BlockSpec((1, 1, page, att), v_map(g)) for g in range(g_pages)]
        ),
        out_specs=pl.BlockSpec((1, ts, att), tok),
        scratch_shapes=[
            pltpu.VMEM((rows, att), F32),
            pltpu.VMEM((nblk, att), F32),
            pltpu.VMEM((npages, rows, page), F32),
            pltpu.VMEM((MOBA_TOPK, rows, 1), jnp.int32),
            pltpu.VMEM((rows, 1), F32),
            pltpu.VMEM((rows, 1), F32),
            pltpu.VMEM((rows, att), F32),
        ],
    )
    return pl.pallas_call(
        functools.partial(_moba_sample_kernel, ts=ts, npages=npages, page=page, att=att),
        grid_spec=grid_spec,
        out_shape=jax.ShapeDtypeStruct((db, ts, att), F32),
        compiler_params=pltpu.CompilerParams(
            dimension_semantics=("arbitrary", "arbitrary", "arbitrary"), vmem_limit_bytes=VMEM_LIMIT),
        name="moba_sample",
    )(page_table, q, k_new, v_new, *([cache_k] * g_pages), *([cache_v] * g_pages))


def _wkv_chunk(r, lw, kk, bb, k, v, hst, cst):
    c = WKV_CHUNK
    ns = WKV_SPLIT
    tri, strict, incl, eye_cat, bdmask, hmasks, hbd = cst
    cum = _mdot([tri], _split(lw, 3))
    cl = cum[c - 1:c, :]
    kkt = kk * jnp.exp(cum - lw)
    rt = r * jnp.exp(cum)
    e_inv = jnp.exp(-cum)
    kh = k * e_inv
    bh = bb * e_inv
    e_rem = jnp.exp(cl - cum)
    kg = k * e_rem
    bg = bb * e_rem

    def stack(parts):
        return [jnp.concatenate([jnp.where(hm, p, jnp.zeros_like(p)) for hm in hmasks], axis=0) for p in parts]

    def bdiag(parts):
        return [jnp.where(bdmask, jnp.concatenate([p] * QUAD_HEADS, axis=0), jnp.zeros((QUAD, QUAD), BF16))
                for p in parts]

    lhs = _split(jnp.concatenate([kkt, rt], axis=0), ns)
    kst = stack(_split(kh, ns))
    bst = stack(_split(bh, ns))
    vst = stack(_split(v, ns))
    a_k = _mdot(lhs, kst, _NT)
    a_b = _mdot(lhs, bst, _NT)
    l_k = jnp.where(strict, a_k[:c], 0.0)
    n_b = jnp.where(strict, -a_b[:c], 0.0)
    a_rk = jnp.where(incl, a_k[c:], 0.0)
    a_rb = jnp.where(incl, a_b[c:], 0.0)

    t_inv = eye_cat + n_b
    pw = n_b
    steps = c.bit_length() - 2
    for _ in range(steps):
        ps = _split(pw, ns)
        pw = _mdot(ps, bdiag(ps))
        t_inv = t_inv + _mdot(_split(t_inv, ns), bdiag(_split(pw, ns)))

    lh = _mdot(lhs, _split(hst, ns))
    w_ = lh[:c] + _mdot(_split(l_k, ns), vst)
    u = _mdot(_split(t_inv, ns), stack(_split(w_, ns)))
    y = lh[c:] + _mdot(_split(a_rk, ns), vst) - _mdot(_split(a_rb, ns), stack(_split(u, ns)))
    gcol = jnp.transpose(jnp.broadcast_to(jnp.exp(cl), (8, QUAD)))[:, 0:1]
    upd = _mdot(_split(jnp.concatenate([kg, -bg], axis=0), ns),
                _split(jnp.concatenate([v, u], axis=0), ns), _TN)
    h_new = hst * gcol + jnp.where(hbd, upd, 0.0)
    return y, h_new


def _rwkv_kernel(rw_ref, sh_ref, h0_ref, mu_ref, w0_ref, wdec_ref, a0_ref, wa_ref, wg_ref, kk_ref, ka_ref,
                 rk_ref, lg_ref, lb_ref, ones_ref, avg_ref, out_ref, hout_ref, h_ref, carry_ref,
                 *, t_valid, rwd):
    t = pl.program_id(1)
    c = WKV_CHUNK
    nq = rwd // QUAD

    @pl.when(t == 0)
    def _():
        h_ref[...] = h0_ref[0]
        carry_ref[...] = sh_ref[0]

    rwf = rw_ref[0]
    row = _iota((c, 1), 0)
    prev = jnp.where(row == 0, carry_ref[...], pltpu.roll(rwf, 1, 0))
    carry_ref[...] = rwf[c - 1:c, :]
    xs = rwf + mu_ref[...] * (prev - rwf)
    r = xs[:, 0:rwd]
    kr = xs[:, rwd:2 * rwd]
    vr = xs[:, 2 * rwd:3 * rwd]
    la = xs[:, 3 * rwd:3 * rwd + LORA_DECAY + LORA_A]
    gd = xs[:, 3 * rwd + LORA_DECAY + LORA_A:]
    w = w0_ref[...] + _mdot(_split(jnp.tanh(la), 2), _split(wdec_ref[...], 2))
    lw = -jnp.exp(-jax.nn.softplus(-w) - 0.5)
    a = jax.nn.sigmoid(a0_ref[...] + _mdot(_split(la, 2), _split(wa_ref[...], 2)))
    g = _mdot(_split(jax.nn.sigmoid(gd), 2), _split(wg_ref[...], 2))
    kk = kr * kk_ref[...]
    km = kr * (1.0 + (a - 1.0) * ka_ref[...])
    ones = ones_ref[...]
    ss = _mdot(_split(kk * kk, 3), [ones])
    kkn = kk / jnp.maximum(jnp.sqrt(ss), 1e-12)
    bb = kkn * a
    if t_valid is not None:
        valid = (t * c + row) < t_valid
        lw = jnp.where(valid, lw, 0.0)
        kkn = jnp.where(valid, kkn, 0.0)
        bb = jnp.where(valid, bb, 0.0)
        km = jnp.where(valid, km, 0.0)
        vr = jnp.where(valid, vr, 0.0)

    lane_c = _iota((c, QUAD_HEADS * c), 1) % c
    row_c = _iota((c, QUAD_HEADS * c), 0)
    tri = (_iota((c, c), 1) <= _iota((c, c), 0)).astype(BF16)
    lane_head = _iota((1, QUAD), 1) // HEAD_DIM
    cst = (
        tri,
        lane_c < row_c,
        lane_c <= row_c,
        (lane_c == row_c).astype(F32),
        (_iota((QUAD_HEADS * c, QUAD_HEADS * c), 0) // c) == (_iota((QUAD_HEADS * c, QUAD_HEADS * c), 1) // c),
        [lane_head == h for h in range(QUAD_HEADS)],
        (_iota((QUAD, QUAD), 0) // HEAD_DIM) == (_iota((QUAD, QUAD), 1) // HEAD_DIM),
    )
    ys = []
    for qd in range(nq):
        sl = slice(qd * QUAD, (qd + 1) * QUAD)
        y, h_new = _wkv_chunk(r[:, sl], lw[:, sl], kkn[:, sl], bb[:, sl], km[:, sl], vr[:, sl], h_ref[qd], cst)
        h_ref[qd] = h_new
        ys.append(y)
    y = jnp.concatenate(ys, axis=1)
    hout_ref[0] = h_ref[...]

    avg = avg_ref[...]
    mean = _mdot(_split(y, 3), [avg])
    d = y - mean
    var = _mdot(_split(d * d, 3), [avg])
    yn = d * lax.rsqrt(var + GN_EPS) * lg_ref[...] + lb_ref[...]
    bonus = _mdot(_split(r * km * rk_ref[...], 3), [ones]) * vr
    out_ref[0] = (yn + bonus) * g


def _rwkv_mix(rw, shift0, h0, prm, t_valid=None):
    b, t, cols = rw.shape
    rwd = prm["k_k"].shape[1]
    nq = rwd // QUAD
    c = WKV_CHUNK
    vec = lambda n: pl.BlockSpec((1, n), lambda bi, ti: (0, 0))
    mat = lambda m, n: pl.BlockSpec((m, n), lambda bi, ti: (0, 0))
    lora_in = LORA_DECAY + LORA_A
    return pl.pallas_call(
        functools.partial(_rwkv_kernel, t_valid=t_valid, rwd=rwd),
        grid=(b, t // c),
        in_specs=[
            pl.BlockSpec((1, c, cols), lambda bi, ti: (bi, ti, 0)),
            pl.BlockSpec((1, 1, cols), lambda bi, ti: (bi, 0, 0)),
            pl.BlockSpec((1, nq, QUAD, QUAD), lambda bi, ti: (bi, 0, 0, 0)),
            vec(cols), vec(rwd), mat(lora_in, rwd), vec(rwd), mat(lora_in, rwd), mat(LORA_GATE, rwd),
            vec(rwd), vec(rwd), vec(rwd), vec(rwd), vec(rwd), mat(rwd, rwd), mat(rwd, rwd),
        ],
        out_specs=[
            pl.BlockSpec((1, c, rwd), lambda bi, ti: (bi, ti, 0)),
            pl.BlockSpec((1, nq, QUAD, QUAD), lambda bi, ti: (bi, 0, 0, 0)),
        ],
        out_shape=[
            jax.ShapeDtypeStruct((b, t, rwd), F32),
            jax.ShapeDtypeStruct((b, nq, QUAD, QUAD), F32),
        ],
        scratch_shapes=[pltpu.VMEM((nq, QUAD, QUAD), F32), pltpu.VMEM((1, cols), F32)],
        compiler_params=pltpu.CompilerParams(
            dimension_semantics=("arbitrary", "arbitrary"), vmem_limit_bytes=VMEM_LIMIT),
        name="rwkv_mix",
    )(rw, shift0, h0, prm["mu"], prm["w0"], prm["wdec"], prm["a0"], prm["wa"], prm["wg"], prm["k_k"],
      prm["k_a"], prm["r_k"], prm["lnx_g"], prm["lnx_b"], prm["ones"], prm["avg"])


def _state_to_blockdiag(s):
    b, h, dv, dk = s.shape
    st = jnp.swapaxes(s, -1, -2).reshape(b, h // QUAD_HEADS, QUAD_HEADS, dk, dv)
    eye = jnp.eye(QUAD_HEADS, dtype=s.dtype)
    return jnp.einsum("bqhkv,hg->bqhkgv", st, eye).reshape(b, h // QUAD_HEADS, QUAD, QUAD)


def _blockdiag_to_state(hb):
    b, nq = hb.shape[:2]
    hr = hb.reshape(b, nq, QUAD_HEADS, HEAD_DIM, QUAD_HEADS, HEAD_DIM)
    st = jnp.einsum("bqhkhv->bqhkv", hr).reshape(b, nq * QUAD_HEADS, HEAD_DIM, HEAD_DIM)
    return jnp.swapaxes(st, -1, -2)


def _out_ffn_kernel(x_ref, a_ref, r_ref, wo_ref, g_ref, wg_ref, wu_ref, wd_ref, o_ref, x1_ref, h2_ref, acc_ref,
                    *, att):
    j = pl.program_id(1)

    @pl.when(j == 0)
    def _():
        x1 = (x_ref[...] + _dot(a_ref[...].astype(BF16), wo_ref[0:att, :])
              + _dot(r_ref[...].astype(BF16), wo_ref[att:, :]))
        x1_ref[...] = x1
        ms = jnp.mean(x1 * x1, axis=-1, keepdims=True)
        h2_ref[...] = (x1 * lax.rsqrt(ms + RMS_EPS) * g_ref[...]).astype(BF16)
        acc_ref[...] = jnp.zeros_like(acc_ref)

    h2 = h2_ref[...]
    gate = _dot(h2, wg_ref[...])
    up = _dot(h2, wu_ref[...])
    act = gate * jax.nn.sigmoid(gate) * up
    acc_ref[...] += _dot(act.astype(BF16), wd_ref[...])

    @pl.when(j == pl.num_programs(1) - 1)
    def _():
        o_ref[...] = x1_ref[...] + acc_ref[...]


def _out_ffn(x2d, attn, rwkv, wo_b, g, wfi_b, wfo_b, tm, hc):
    n, d = x2d.shape
    att = attn.shape[1]
    hid = wfo_b.shape[0]
    nh = hid // hc
    row = lambda i, j: (i, 0)
    const = lambda i, j: (0, 0)
    return pl.pallas_call(
        functools.partial(_out_ffn_kernel, att=att),
        grid=(n // tm, nh),
        in_specs=[
            pl.BlockSpec((tm, d), row),
            pl.BlockSpec((tm, att), row),
            pl.BlockSpec((tm, rwkv.shape[1]), row),
            pl.BlockSpec(wo_b.shape, const),
            pl.BlockSpec((1, d), const),
            pl.BlockSpec((d, hc), lambda i, j: (0, j)),
            pl.BlockSpec((d, hc), lambda i, j: (0, j + nh)),
            pl.BlockSpec((hc, d), lambda i, j: (j, 0)),
        ],
        out_specs=pl.BlockSpec((tm, d), row),
        out_shape=jax.ShapeDtypeStruct((n, d), F32),
        scratch_shapes=[pltpu.VMEM((tm, d), F32), pltpu.VMEM((tm, d), BF16), pltpu.VMEM((tm, d), F32)],
        compiler_params=pltpu.CompilerParams(
            dimension_semantics=("arbitrary", "arbitrary"), vmem_limit_bytes=VMEM_LIMIT),
        name="out_ffn",
    )(x2d, attn, rwkv, wo_b, g, wfi_b, wfi_b, wfo_b)


def _rms_kernel(x_ref, g_ref, o_ref):
    x = x_ref[...]
    ms = jnp.mean(x * x, axis=-1, keepdims=True)
    o_ref[...] = x * lax.rsqrt(ms + RMS_EPS) * g_ref[...]


def _rms_norm(x2d, g, tm):
    n, d = x2d.shape
    return pl.pallas_call(
        _rms_kernel,
        grid=(n // tm,),
        in_specs=[pl.BlockSpec((tm, d), lambda i: (i, 0)), pl.BlockSpec((1, d), lambda i: (0, 0))],
        out_specs=pl.BlockSpec((tm, d), lambda i: (i, 0)),
        out_shape=jax.ShapeDtypeStruct((n, d), F32),
        name="rms_norm",
    )(x2d, g)


def _ffn_chunk(hid):
    best = 128
    for hc in range(128, hid // 2 + 1, 128):
        if hid % hc == 0:
            best = hc
    return best


def kernel(x_prompt, x_sample, cache_k, cache_v, state_wkv, state_shift, page_table, norm_attn_g, w_in, shift_mu, decay_w0, decay_up, iclr_a0, iclr_up, gate_up, k_k, k_a, r_k, lnx_g, lnx_b, w_out, norm_ffn_g, w_ffn_in, w_ffn_out, norm_final_g):
    bsz, seq, d = x_prompt.shape
    db, ts, _ = x_sample.shape
    depth = w_in.shape[0]
    rwd = k_k.shape[1]
    att = (w_in.shape[2] - 3 * rwd - LORA_DECAY - LORA_A - LORA_GATE) // 3
    nheads_att = att // HEAD_DIM
    nheads_rw = rwd // HEAD_DIM
    cols = 3 * rwd + LORA_DECAY + LORA_A + LORA_GATE
    pool, page = cache_k.shape[1], cache_k.shape[2]
    npages = page_table.shape[1]
    past_len = npages * page
    hid = w_ffn_out.shape[1]
    hc = _ffn_chunk(hid)

    np_tok = bsz * seq
    ns_tok = db * ts
    tm_p = 256
    tabs_p = _rope_tables(jnp.arange(seq, dtype=jnp.int32), att)
    pos_s = past_len + jnp.arange(ts, dtype=jnp.int32)
    tabs_s = tuple(jnp.tile(a, (db, 1)) for a in _rope_tables(pos_s, att))

    ck = cache_k.reshape(depth, pool, page, att)
    cv = cache_v.reshape(depth, pool, page, att)

    head_of = jnp.arange(rwd) // HEAD_DIM
    same_head = head_of[:, None] == head_of[None, :]
    ones_bd = same_head.astype(BF16)
    avg_bd = (same_head.astype(F32) / HEAD_DIM).astype(BF16)

    ts_pad = WKV_CHUNK
    zeros_shift = jnp.zeros((bsz, 1, cols), F32)
    zeros_state = jnp.zeros((bsz, nheads_rw // QUAD_HEADS, QUAD, QUAD), F32)

    xp = x_prompt.reshape(np_tok, d)
    xs = x_sample.reshape(ns_tok, d)
    outs = {n: [] for n in ("kp", "vp", "sp", "hp", "ks", "vs", "ss", "hs")}
    for l in range(depth):
        w_in_b = w_in[l].astype(BF16)
        wo_b = w_out[l].astype(BF16)
        wfi_b = w_ffn_in[l].astype(BF16)
        wfo_b = w_ffn_out[l].astype(BF16)
        g_attn = norm_attn_g[l].reshape(1, d)
        g_ffn = norm_ffn_g[l].reshape(1, d)
        prm = {
            "mu": shift_mu[l].reshape(1, cols),
            "w0": decay_w0[l].reshape(1, rwd),
            "wdec": jnp.concatenate([decay_up[l], jnp.zeros((LORA_A, rwd), F32)], axis=0),
            "a0": iclr_a0[l].reshape(1, rwd),
            "wa": jnp.concatenate([jnp.zeros((LORA_DECAY, rwd), F32), iclr_up[l]], axis=0),
            "wg": gate_up[l],
            "k_k": k_k[l].reshape(1, rwd),
            "k_a": k_a[l].reshape(1, rwd),
            "r_k": r_k[l].reshape(1, rwd),
            "lnx_g": lnx_g[l].reshape(1, rwd),
            "lnx_b": lnx_b[l].reshape(1, rwd),
            "ones": ones_bd,
            "avg": avg_bd,
        }

        q, k, v, rw = _norm_proj(xp, g_attn, w_in_b, tabs_p, tm_p, att)
        attn = _moba_prompt(q.reshape(bsz, seq, att), k.reshape(bsz, seq, att), v.reshape(bsz, seq, att))
        rw3 = rw.reshape(bsz, seq, cols)
        mix, hfin = _rwkv_mix(rw3, zeros_shift, zeros_state, prm)
        xp = _out_ffn(xp, attn.reshape(np_tok, att), mix.reshape(np_tok, rwd), wo_b, g_ffn, wfi_b, wfo_b, tm_p, hc)
        outs["kp"].append(k.reshape(bsz, seq, nheads_att, HEAD_DIM))
        outs["vp"].append(v.reshape(bsz, seq, nheads_att, HEAD_DIM))
        outs["sp"].append(_blockdiag_to_state(hfin))
        outs["hp"].append(rw3[:, -1])

        q, k, v, rw = _norm_proj(xs, g_attn, w_in_b, tabs_s, ns_tok, att)
        q3, k3, v3 = (a.reshape(db, ts, att) for a in (q, k, v))
        attn = _moba_sample(q3, k3, v3, ck, cv, page_table, l)
        rw3 = rw.reshape(db, ts, cols)
        rw_pad = jnp.pad(rw3, ((0, 0), (0, ts_pad - ts), (0, 0)))
        mix, hfin = _rwkv_mix(rw_pad, state_shift[l].reshape(db, 1, cols), _state_to_blockdiag(state_wkv[l]),
                              prm, t_valid=ts)
        mix = mix[:, :ts].reshape(ns_tok, rwd)
        xs = _out_ffn(xs, attn.reshape(ns_tok, att), mix, wo_b, g_ffn, wfi_b, wfo_b, ns_tok, hc)
        outs["ks"].append(k3.reshape(db, ts, nheads_att, HEAD_DIM))
        outs["vs"].append(v3.reshape(db, ts, nheads_att, HEAD_DIM))
        outs["ss"].append(_blockdiag_to_state(hfin))
        outs["hs"].append(rw3[:, -1])

    g_fin = norm_final_g.reshape(1, d)
    y_prompt = _rms_norm(xp, g_fin, tm_p).reshape(bsz, seq, d)
    y_sample = _rms_norm(xs, g_fin, ns_tok).reshape(db, ts, d)
    st = {n: jnp.stack(v) for n, v in outs.items()}
    return (y_prompt, y_sample, st["kp"], st["vp"], st["sp"], st["hp"], st["ks"], st["vs"], st["ss"], st["hs"])
```

```python
import functools

import jax
import jax.numpy as jnp
from jax import lax
from jax.experimental import pallas as pl
from jax.experimental.pallas import tpu as pltpu

F32 = jnp.float32
BF16 = jnp.bfloat16

HEAD_DIM = 64
MOBA_BLOCK = 256
MOBA_TOPK = 3
ROT_DIM = HEAD_DIM // 4
ROPE_THETA = 500000.0
LORA_DECAY = 64
LORA_A = 64
LORA_GATE = 128
RMS_EPS = 1e-6
GN_EPS = 64e-5
NEG_INF = -1e30

QUAD_HEADS = 4
QUAD = QUAD_HEADS * HEAD_DIM
WKV_CHUNK = 64
WKV_TILE = 256
STATE_SPLIT = 2
SUBLANES = 8
VMEM_LIMIT = 48 * 1024 * 1024

_NN = (((1,), (0,)), ((), ()))
_NT = (((1,), (1,)), ((), ()))
_TN = (((0,), (0,)), ((), ()))


def _split(x, n):
    parts = []
    for i in range(n):
        p = x.astype(BF16)
        parts.append(p)
        if i + 1 < n:
            x = x - p.astype(F32)
    return parts


def _mdot(pa, pb, dims=_NN):
    order = max(len(pa), len(pb))
    acc = None
    for i in reversed(range(len(pa))):
        for j in reversed(range(len(pb))):
            if i + j < order:
                t = lax.dot_general(pa[i], pb[j], dims, preferred_element_type=F32)
                acc = t if acc is None else acc + t
    return acc


def _dot(a, b, dims=_NN):
    return lax.dot_general(a, b, dims, preferred_element_type=F32)


def _iota(shape, dim):
    return lax.broadcasted_iota(jnp.int32, shape, dim)


def _top3(gate, nvalid, axis):
    blk = _iota(gate.shape, axis)
    g = jnp.where(blk < nvalid, gate, NEG_INF)
    sels = []
    for _ in range(MOBA_TOPK):
        m = jnp.max(g, axis=axis, keepdims=True)
        idx = jnp.min(jnp.where(g == m, blk, jnp.int32(1 << 30)), axis=axis, keepdims=True)
        sels.append(jnp.where(m > 0.5 * NEG_INF, idx, -1))
        g = jnp.where(blk == idx, NEG_INF, g)
    return sels


def _norm_proj_kernel(x_ref, g_ref, w_ref, cos_ref, sna_ref, snb_ref, q_ref, k_ref, v_ref, rw_ref, *, att):
    x = x_ref[...]
    ms = jnp.mean(x * x, axis=-1, keepdims=True)
    h = (x * lax.rsqrt(ms + RMS_EPS) * g_ref[...]).astype(BF16)
    cos, sna, snb = cos_ref[...], sna_ref[...], snb_ref[...]
    half = ROT_DIM // 2

    def rope(p):
        return p * cos + pltpu.roll(p, att - half, 1) * sna + pltpu.roll(p, half, 1) * snb

    q_ref[...] = rope(_dot(h, w_ref[:, 0:att]))
    k_ref[...] = rope(_dot(h, w_ref[:, att:2 * att]))
    v_ref[...] = _dot(h, w_ref[:, 2 * att:3 * att])
    rw_ref[...] = _dot(h, w_ref[:, 3 * att:])


def _norm_proj(x2d, g, w_b, tabs, tm, att):
    n, d = x2d.shape
    cols = w_b.shape[1]
    rwc = cols - 3 * att
    ntab = tabs[0].shape[0] // tm
    row = lambda i: (i, 0)
    tab = lambda i: (i % ntab, 0)
    const = lambda i: (0, 0)
    return pl.pallas_call(
        functools.partial(_norm_proj_kernel, att=att),
        grid=(n // tm,),
        in_specs=[
            pl.BlockSpec((tm, d), row),
            pl.BlockSpec((1, d), const),
            pl.BlockSpec((d, cols), const),
            pl.BlockSpec((tm, att), tab),
            pl.BlockSpec((tm, att), tab),
            pl.BlockSpec((tm, att), tab),
        ],
        out_specs=[
            pl.BlockSpec((tm, att), row),
            pl.BlockSpec((tm, att), row),
            pl.BlockSpec((tm, att), row),
            pl.BlockSpec((tm, rwc), row),
        ],
        out_shape=[
            jax.ShapeDtypeStruct((n, att), F32),
            jax.ShapeDtypeStruct((n, att), F32),
            jax.ShapeDtypeStruct((n, att), F32),
            jax.ShapeDtypeStruct((n, rwc), F32),
        ],
        compiler_params=pltpu.CompilerParams(
            dimension_semantics=("arbitrary",), vmem_limit_bytes=VMEM_LIMIT),
        name="norm_proj",
    )(x2d, g, w_b, *tabs)


def _rope_tables(pos, att):
    half = ROT_DIM // 2
    inv = jnp.float32(ROPE_THETA) ** (-jnp.arange(half, dtype=F32) / half)
    ang = pos.astype(F32)[:, None] * inv[None, :]
    cos, sin = jnp.cos(ang), jnp.sin(ang)
    t = pos.shape[0]
    rest = HEAD_DIM - ROT_DIM
    cos_h = jnp.concatenate([cos, cos, jnp.ones((t, rest), F32)], axis=1)
    sna_h = jnp.concatenate([-sin, jnp.zeros((t, half + rest), F32)], axis=1)
    snb_h = jnp.concatenate([jnp.zeros((t, half), F32), sin, jnp.zeros((t, rest), F32)], axis=1)
    nh = att // HEAD_DIM
    return tuple(jnp.tile(a, (1, nh)) for a in (cos_h, sna_h, snb_h))


def _moba_prompt_kernel(qt_ref, k_ref, vt_ref, o_ref, kb_ref, kmean_ref, *, nb):
    i = pl.program_id(2)
    blk = MOBA_BLOCK

    @pl.when(i == 0)
    def _():
        for j in range(nb):
            kj = k_ref[0, j * blk:(j + 1) * blk, :]
            kmean_ref[j:j + 1, :] = jnp.mean(kj, axis=0, keepdims=True)
            kb_ref[j * blk:(j + 1) * blk, :] = kj.astype(BF16)

    qt = qt_ref[0, 0] * (HEAD_DIM ** -0.5)
    row_head = _iota((QUAD, 1), 0) // HEAD_DIM
    causal = _iota((blk, blk), 0) <= _iota((blk, blk), 1)
    kmean = _split(kmean_ref[...], 2)
    k_own = kb_ref[pl.ds(pl.multiple_of(i * blk, blk), blk), :]

    head_rows = [slice(h * HEAD_DIM, (h + 1) * HEAD_DIM) for h in range(QUAD_HEADS)]
    qmbs, sels, state = [], [], []
    for h in range(QUAD_HEADS):
        qm = jnp.where(row_head == h, qt, 0.0)
        sels.append(_top3(_mdot(kmean, _split(qm, 2)), i, axis=0))
        qmb = qm.astype(BF16)
        qmbs.append(qmb)
        s = jnp.where(causal, _dot(k_own, qmb), NEG_INF)
        m = jnp.max(s, axis=0, keepdims=True)
        p = jnp.exp(s - m)
        l = jnp.sum(p, axis=0, keepdims=True)
        state += [m, l, _dot(vt_ref[0, i, head_rows[h], :], p.astype(BF16))]

    def body(j, carry):
        kj = kb_ref[pl.ds(pl.multiple_of(j * blk, blk), blk), :]
        new = []
        for h in range(QUAD_HEADS):
            m, l, acc = carry[3 * h:3 * h + 3]
            s1, s2, s3 = sels[h]
            picked = (s1 == j) | (s2 == j) | (s3 == j)
            s = jnp.where(picked, _dot(kj, qmbs[h]), NEG_INF)
            m_new = jnp.maximum(m, jnp.max(s, axis=0, keepdims=True))
            alpha = jnp.exp(m - m_new)
            p = jnp.exp(s - m_new)
            l = alpha * l + jnp.sum(p, axis=0, keepdims=True)
            acc = acc * alpha + _dot(vt_ref[0, j, head_rows[h], :], p.astype(BF16))
            new += [m_new, l, acc]
        return tuple(new)

    state = lax.fori_loop(0, i, body, tuple(state))
    o_ref[0, 0] = jnp.concatenate([state[3 * h + 2] / state[3 * h + 1] for h in range(QUAD_HEADS)], axis=0)


def _moba_prompt(q, k, v):
    b, t, att = q.shape
    nb = t // MOBA_BLOCK
    nq = att // QUAD
    blk = MOBA_BLOCK
    qt = q.reshape(b, nb, blk, att).transpose(0, 1, 3, 2)
    vt = v.astype(BF16).reshape(b, nb, blk, att).transpose(0, 1, 3, 2)
    ot = pl.pallas_call(
        functools.partial(_moba_prompt_kernel, nb=nb),
        grid=(b, nq, nb),
        in_specs=[
            pl.BlockSpec((1, 1, QUAD, blk), lambda bi, qi, i: (bi, i, qi, 0)),
            pl.BlockSpec((1, t, QUAD), lambda bi, qi, i: (bi, 0, qi)),
            pl.BlockSpec((1, nb, QUAD, blk), lambda bi, qi, i: (bi, 0, qi, 0)),
        ],
        out_specs=pl.BlockSpec((1, 1, QUAD, blk), lambda bi, qi, i: (bi, i, qi, 0)),
        out_shape=jax.ShapeDtypeStruct((b, nb, att, blk), F32),
        scratch_shapes=[
            pltpu.VMEM((t, QUAD), BF16),
            pltpu.VMEM((nb, QUAD), F32),
        ],
        compiler_params=pltpu.CompilerParams(
            dimension_semantics=("arbitrary", "arbitrary", "arbitrary"), vmem_limit_bytes=VMEM_LIMIT),
        name="moba_prompt",
    )(qt, k, vt)
    return ot.transpose(0, 1, 3, 2).reshape(b, t, att)


PAGES_PER_STEP = 8


def _moba_sample_kernel(pt_ref, qh_ref, qt_ref, kn_ref, vn_ref, *refs, ts, nh, npages, page):
    g_pages = PAGES_PER_STEP
    k_refs = refs[:g_pages]
    v_refs = refs[g_pages:2 * g_pages]
    o_ref = refs[2 * g_pages]
    qb_ref, s_ref, sel_ref, m_ref, l_ref, own_ref, acc_ref = refs[2 * g_pages + 1:]
    ph = pl.program_id(1)
    p = pl.program_id(2)
    nsteps = npages // g_pages
    rows = nh * ts
    ppb = MOBA_BLOCK // page
    nblk = npages // ppb
    scale = HEAD_DIM ** -0.5
    groups = HEAD_DIM // SUBLANES

    @pl.when((ph == 0) & (p == 0))
    def _():
        qt = qt_ref[0] * scale
        for r in range(rows):
            qb_ref[r] = jnp.broadcast_to(qt[:, r:r + 1], (HEAD_DIM, page))
        acc_ref[...] = jnp.zeros_like(acc_ref)

    @pl.when(ph == 0)
    def _():
        fold = (_iota((rows, rows * SUBLANES), 1) // SUBLANES == _iota((rows, rows * SUBLANES), 0)).astype(BF16)
        for g in range(g_pages):
            parts = []
            for h in range(nh):
                kt = k_refs[g][0, 0, h]
                for t in range(ts):
                    prod = kt * qb_ref[h * ts + t]
                    part = prod[0:SUBLANES]
                    for u in range(1, groups):
                        part = part + prod[u * SUBLANES:(u + 1) * SUBLANES]
                    parts.append(part)
            stacked = jnp.concatenate(parts, axis=0)
            s_ref[p * g_pages + g] = _mdot([fold], _split(stacked, 3))

    @pl.when((ph == 0) & (p == nsteps - 1))
    def _():
        lane = _iota((rows, page), 1)

        def gate_sum(pg, gate):
            rs = jnp.sum(s_ref[pg], axis=1, keepdims=True)
            return gate + jnp.where(lane == pg // ppb, rs, 0.0)

        gate = lax.fori_loop(0, npages, gate_sum, jnp.zeros((rows, page), F32))
        sels = _top3(gate, nblk, axis=1)
        for r in range(MOBA_TOPK):
            sel_ref[r] = sels[r]

        def picked(pg):
            b_ = pg // ppb
            return (sels[0] == b_) | (sels[1] == b_) | (sels[2] == b_)

        def mx(pg, m):
            return jnp.maximum(m, jnp.max(jnp.where(picked(pg), s_ref[pg], NEG_INF), axis=1, keepdims=True))

        m = lax.fori_loop(0, npages, mx, jnp.full((rows, 1), NEG_INF, F32))
        qh = qh_ref[0] * scale
        row_t = _iota((rows, 1), 0) % ts
        s_own = []
        for t in range(ts):
            st = jnp.sum(qh * kn_ref[0, t], axis=1, keepdims=True)
            st = jnp.where(row_t >= t, st, NEG_INF)
            s_own.append(st)
            m = jnp.maximum(m, st)

        def weights(pg, l):
            pe = jnp.where(picked(pg), jnp.exp(s_ref[pg] - m), 0.0)
            s_ref[pg] = pe
            return l + jnp.sum(pe, axis=1, keepdims=True)

        l = lax.fori_loop(0, npages, weights, jnp.zeros((rows, 1), F32))
        own = jnp.zeros((rows, HEAD_DIM), F32)
        for t in range(ts):
            pt_ = jnp.exp(s_own[t] - m)
            l = l + pt_
            own = own + pt_ * vn_ref[0, t]
        l_ref[...] = l
        own_ref[...] = own

    @pl.when(ph == 1)
    def _():
        for h in range(nh):
            parts = [None] * ts
            for g in range(g_pages):
                vt = v_refs[g][0, 0, h]
                for t in range(ts):
                    r = h * ts + t
                    term = vt * s_ref[p * g_pages + g, r:r + 1, :]
                    parts[t] = term if parts[t] is None else parts[t] + term
            for t in range(ts):
                acc_ref[h * ts + t] += parts[t]

    @pl.when((ph == 1) & (p == nsteps - 1))
    def _():
        lane = _iota((HEAD_DIM, page), 1)
        cols = jnp.zeros((HEAD_DIM, page), F32)
        for r in range(rows):
            cols = jnp.where(lane == r, jnp.sum(acc_ref[r], axis=1, keepdims=True), cols)
        past = jnp.transpose(cols)[0:rows, :]
        o_ref[0] = (past + own_ref[...]) / l_ref[...]


def _moba_sample(q, k_new, v_new, cache_k, cache_v, page_table, layer):
    db, ts, att = q.shape
    npages = page_table.shape[1]
    page = cache_k.shape[2]
    g_pages = PAGES_PER_STEP
    nsteps = npages // g_pages
    nh = att // HEAD_DIM
    rows = ts * nh
    assert npages % g_pages == 0 and MOBA_BLOCK % page == 0 and rows <= page
    assert npages * page // MOBA_BLOCK <= page

    to_rows = lambda a: a.reshape(db, ts, nh, HEAD_DIM).transpose(0, 2, 1, 3).reshape(db, rows, HEAD_DIM)
    qh = to_rows(q)
    qt = qh.transpose(0, 2, 1)
    rep = lambda a: jnp.broadcast_to(a.reshape(db, ts, nh, 1, HEAD_DIM),
                                     (db, ts, nh, ts, HEAD_DIM)).reshape(db, ts, rows, HEAD_DIM)
    kn = rep(k_new)
    vn = rep(v_new)
    ckt = cache_k.transpose(0, 1, 3, 4, 2)
    cvt = cache_v.transpose(0, 1, 3, 4, 2)

    def k_map(g):
        def f(b, ph, p, pt):
            pg = jnp.where(ph == 0, p, nsteps - 1) * g_pages + g
            return (layer, pt[b, pg], 0, 0, 0)
        return f

    def v_map(g):
        def f(b, ph, p, pt):
            pg = jnp.where(ph == 0, 0, p) * g_pages + g
            return (layer, pt[b, pg], 0, 0, 0)
        return f

    tok3 = lambda b, ph, p, pt: (b, 0, 0)
    tok4 = lambda b, ph, p, pt: (b, 0, 0, 0)
    grid_spec = pltpu.PrefetchScalarGridSpec(
        num_scalar_prefetch=1,
        grid=(db, 2, nsteps),
        in_specs=(
            [pl.BlockSpec((1, rows, HEAD_DIM), tok3), pl.BlockSpec((1, HEAD_DIM, rows), tok3),
             pl.BlockSpec((1, ts, rows, HEAD_DIM), tok4), pl.BlockSpec((1, ts, rows, HEAD_DIM), tok4)]
            + [pl.BlockSpec((1, 1, nh, HEAD_DIM, page), k_map(g)) for g in range(g_pages)]
            + [pl.BlockSpec((1, 1, nh, HEAD_DIM, page), v_map(g)) for g in range(g_pages)]
        ),
        out_specs=pl.BlockSpec((1, rows, HEAD_DIM), tok3),
        scratch_shapes=[
            pltpu.VMEM((rows, HEAD_DIM, page), F32),
            pltpu.VMEM((npages, rows, page), F32),
            pltpu.VMEM((MOBA_TOPK, rows, 1), jnp.int32),
            pltpu.VMEM((rows, 1), F32),
            pltpu.VMEM((rows, 1), F32),
            pltpu.VMEM((rows, HEAD_DIM), F32),
            pltpu.VMEM((rows, HEAD_DIM, page), F32),
        ],
    )
    o = pl.pallas_call(
        functools.partial(_moba_sample_kernel, ts=ts, nh=nh, npages=npages, page=page),
        grid_spec=grid_spec,
        out_shape=jax.ShapeDtypeStruct((db, rows, HEAD_DIM), F32),
        compiler_params=pltpu.CompilerParams(
            dimension_semantics=("arbitrary", "arbitrary", "arbitrary"), vmem_limit_bytes=VMEM_LIMIT),
        name="moba_sample",
    )(page_table, qh, qt, kn, vn, *([ckt] * g_pages), *([cvt] * g_pages))
    return o.reshape(db, nh, ts, HEAD_DIM).transpose(0, 2, 1, 3).reshape(db, ts, att)


def _wkv_local(r, lw, kk, bb, k, v, cst):
    c = WKV_CHUNK
    tri, strict, incl, eye_cat, bdmask, hmasks, hbd, eye_q = cst
    cum = _mdot([tri], _split(lw, 3))
    cl = cum[c - 1:c, :]
    kkt = kk * jnp.exp(cum - lw)
    rt = r * jnp.exp(cum)
    e_inv = jnp.exp(-cum)
    kh = k * e_inv
    bh = bb * e_inv
    e_rem = jnp.exp(cl - cum)
    kg = k * e_rem
    bg = bb * e_rem

    def stack(x):
        xb = x.astype(BF16)
        return jnp.concatenate([jnp.where(hm, xb, jnp.zeros_like(xb)) for hm in hmasks], axis=0)

    def bdiag(x):
        xb = x.astype(BF16)
        return jnp.where(bdmask, jnp.concatenate([xb] * QUAD_HEADS, axis=0), jnp.zeros((QUAD, QUAD), BF16))

    lhs = jnp.concatenate([kkt, rt], axis=0).astype(BF16)
    vst = stack(v)
    a_k = _dot(lhs, stack(kh), _NT)
    a_b = _dot(lhs, stack(bh), _NT)
    l_k = jnp.where(strict, a_k[:c], 0.0)
    n_b = jnp.where(strict, -a_b[:c], 0.0)
    a_rk = jnp.where(incl, a_k[c:], 0.0).astype(BF16)
    a_rb = jnp.where(incl, a_b[c:], 0.0).astype(BF16)

    t_inv = eye_cat + n_b
    pw = n_b
    for _ in range(c.bit_length() - 2):
        pw = _dot(pw.astype(BF16), bdiag(pw))
        t_inv = t_inv + _dot(t_inv.astype(BF16), bdiag(pw))
    t_b = t_inv.astype(BF16)

    lkv = _dot(l_k.astype(BF16), vst)
    uk = _dot(t_b, stack(kkt))
    uc = _dot(t_b, stack(lkv))
    ry = rt - _dot(a_rb, stack(uk))
    yc = _dot(a_rk, vst) - _dot(a_rb, stack(uc))
    gcol = jnp.transpose(jnp.broadcast_to(jnp.exp(cl), (SUBLANES, QUAD)))[:, 0:1]
    bgb = bg.astype(BF16)
    m_mat = jnp.where(eye_q, gcol, 0.0) - jnp.where(hbd, _dot(bgb, uk.astype(BF16), _TN), 0.0)
    n_mat = jnp.where(hbd, _dot(jnp.concatenate([kg.astype(BF16), -bgb], axis=0),
                                jnp.concatenate([v, uc], axis=0).astype(BF16), _TN), 0.0)
    return m_mat, n_mat, ry, yc


def _rwkv_kernel(rw_ref, sh_ref, h0_ref, mu_ref, w0_ref, wdec_ref, a0_ref, wa_ref, wg_ref, kk_ref, ka_ref,
                 rk_ref, lg_ref, lb_ref, ones_ref, avg_ref, out_ref, hout_ref,
                 h_ref, carry_ref, m_ref, n_ref, ry_ref, yc_ref, *, tt, t_valid, rwd):
    t = pl.program_id(1)
    c = WKV_CHUNK
    nq = rwd // QUAD
    nch = tt // c

    @pl.when(t == 0)
    def _():
        h_ref[...] = h0_ref[0]
        carry_ref[...] = sh_ref[0]

    rwf = rw_ref[0]
    row = _iota((tt, 1), 0)
    prev = jnp.where(row == 0, carry_ref[...], pltpu.roll(rwf, 1, 0))
    carry_ref[...] = rwf[tt - 1:tt, :]
    xs = rwf + mu_ref[...] * (prev - rwf)
    r = xs[:, 0:rwd]
    kr = xs[:, rwd:2 * rwd]
    vr = xs[:, 2 * rwd:3 * rwd]
    la = xs[:, 3 * rwd:3 * rwd + LORA_DECAY + LORA_A]
    gd = xs[:, 3 * rwd + LORA_DECAY + LORA_A:]
    w = w0_ref[...] + _mdot(_split(jnp.tanh(la), 2), _split(wdec_ref[...], 2))
    lw = -jnp.exp(-jax.nn.softplus(-w) - 0.5)
    a = jax.nn.sigmoid(a0_ref[...] + _mdot(_split(la, 2), _split(wa_ref[...], 2)))
    g = _mdot(_split(jax.nn.sigmoid(gd), 2), _split(wg_ref[...], 2))
    kk = kr * kk_ref[...]
    km = kr * (1.0 + (a - 1.0) * ka_ref[...])
    ones = ones_ref[...]
    ss = _mdot(_split(kk * kk, 3), [ones])
    kkn = kk / jnp.maximum(jnp.sqrt(ss), 1e-12)
    bb = kkn * a
    if t_valid is not None:
        valid = (t * tt + row) < t_valid
        lw = jnp.where(valid, lw, 0.0)
        kkn = jnp.where(valid, kkn, 0.0)
        bb = jnp.where(valid, bb, 0.0)
        km = jnp.where(valid, km, 0.0)
        vr = jnp.where(valid, vr, 0.0)

    lane_c = _iota((c, QUAD_HEADS * c), 1) % c
    row_c = _iota((c, QUAD_HEADS * c), 0)
    lane_head = _iota((1, QUAD), 1) // HEAD_DIM
    cst = (
        (_iota((c, c), 1) <= _iota((c, c), 0)).astype(BF16),
        lane_c < row_c,
        lane_c <= row_c,
        (lane_c == row_c).astype(F32),
        (_iota((QUAD_HEADS * c, QUAD_HEADS * c), 0) // c) == (_iota((QUAD_HEADS * c, QUAD_HEADS * c), 1) // c),
        [lane_head == h for h in range(QUAD_HEADS)],
        (_iota((QUAD, QUAD), 0) // HEAD_DIM) == (_iota((QUAD, QUAD), 1) // HEAD_DIM),
        _iota((QUAD, QUAD), 0) == _iota((QUAD, QUAD), 1),
    )
    for ci in range(nch):
        rs = slice(ci * c, (ci + 1) * c)
        for qd in range(nq):
            sl = slice(qd * QUAD, (qd + 1) * QUAD)
            m_mat, n_mat, ry, yc = _wkv_local(r[rs, sl], lw[rs, sl], kkn[rs, sl], bb[rs, sl], km[rs, sl],
                                              vr[rs, sl], cst)
            u = ci * nq + qd
            m_ref[u] = m_mat
            n_ref[u] = n_mat
            ry_ref[u] = ry
            yc_ref[u] = yc
    ycols = []
    for qd in range(nq):
        hst = h_ref[qd]
        ys = []
        for ci in range(nch):
            u = ci * nq + qd
            hs = _split(hst, STATE_SPLIT)
            ys.append(_mdot(_split(ry_ref[u], STATE_SPLIT), hs) + yc_ref[u])
            hst = _mdot(_split(m_ref[u], STATE_SPLIT), hs) + n_ref[u]
        h_ref[qd] = hst
        ycols.append(jnp.concatenate(ys, axis=0) if nch > 1 else ys[0])
    y = jnp.concatenate(ycols, axis=1)
    hout_ref[0] = h_ref[...]

    avg = avg_ref[...]
    mean = _mdot(_split(y, 3), [avg])
    d = y - mean
    var = _mdot(_split(d * d, 3), [avg])
    yn = d * lax.rsqrt(var + GN_EPS) * lg_ref[...] + lb_ref[...]
    bonus = _mdot(_split(r * km * rk_ref[...], 3), [ones]) * vr
    out_ref[0] = (yn + bonus) * g


def _rwkv_mix(rw, shift0, h0, prm, tt, t_valid=None):
    b, t, cols = rw.shape
    rwd = prm["k_k"].shape[1]
    nq = rwd // QUAD
    c = WKV_CHUNK
    ninst = (tt // c) * nq
    vec = lambda n: pl.BlockSpec((1, n), lambda bi, ti: (0, 0))
    mat = lambda m, n: pl.BlockSpec((m, n), lambda bi, ti: (0, 0))
    lora_in = LORA_DECAY + LORA_A
    return pl.pallas_call(
        functools.partial(_rwkv_kernel, tt=tt, t_valid=t_valid, rwd=rwd),
        grid=(b, t // tt),
        in_specs=[
            pl.BlockSpec((1, tt, cols), lambda bi, ti: (bi, ti, 0)),
            pl.BlockSpec((1, 1, cols), lambda bi, ti: (bi, 0, 0)),
            pl.BlockSpec((1, nq, QUAD, QUAD), lambda bi, ti: (bi, 0, 0, 0)),
            vec(cols), vec(rwd), mat(lora_in, rwd), vec(rwd), mat(lora_in, rwd), mat(LORA_GATE, rwd),
            vec(rwd), vec(rwd), vec(rwd), vec(rwd), vec(rwd), mat(rwd, rwd), mat(rwd, rwd),
        ],
        out_specs=[
            pl.BlockSpec((1, tt, rwd), lambda bi, ti: (bi, ti, 0)),
            pl.BlockSpec((1, nq, QUAD, QUAD), lambda bi, ti: (bi, 0, 0, 0)),
        ],
        out_shape=[
            jax.ShapeDtypeStruct((b, t, rwd), F32),
            jax.ShapeDtypeStruct((b, nq, QUAD, QUAD), F32),
        ],
        scratch_shapes=[
            pltpu.VMEM((nq, QUAD, QUAD), F32),
            pltpu.VMEM((1, cols), F32),
            pltpu.VMEM((ninst, QUAD, QUAD), F32),
            pltpu.VMEM((ninst, QUAD, QUAD), F32),
            pltpu.VMEM((ninst, c, QUAD), F32),
            pltpu.VMEM((ninst, c, QUAD), F32),
        ],
        compiler_params=pltpu.CompilerParams(
            dimension_semantics=("arbitrary", "arbitrary"), vmem_limit_bytes=VMEM_LIMIT),
        name="rwkv_mix",
    )(rw, shift0, h0, prm["mu"], prm["w0"], prm["wdec"], prm["a0"], prm["wa"], prm["wg"], prm["k_k"],
      prm["k_a"], prm["r_k"], prm["lnx_g"], prm["lnx_b"], prm["ones"], prm["avg"])


def _state_to_blockdiag(s):
    b, h, dv, dk = s.shape
    st = jnp.swapaxes(s, -1, -2).reshape(b, h // QUAD_HEADS, QUAD_HEADS, dk, dv)
    eye = jnp.eye(QUAD_HEADS, dtype=s.dtype)
    return jnp.einsum("bqhkv,hg->bqhkgv", st, eye).reshape(b, h // QUAD_HEADS, QUAD, QUAD)


def _blockdiag_to_state(hb):
    b, nq = hb.shape[:2]
    hr = hb.reshape(b, nq, QUAD_HEADS, HEAD_DIM, QUAD_HEADS, HEAD_DIM)
    st = jnp.einsum("bqhkhv->bqhkv", hr).reshape(b, nq * QUAD_HEADS, HEAD_DIM, HEAD_DIM)
    return jnp.swapaxes(st, -1, -2)


def _out_ffn_kernel(x_ref, a_ref, r_ref, wo_ref, g_ref, wg_ref, wu_ref, wd_ref, o_ref, x1_ref, h2_ref, acc_ref,
                    *, att):
    j = pl.program_id(1)

    @pl.when(j == 0)
    def _():
        x1 = (x_ref[...] + _dot(a_ref[...].astype(BF16), wo_ref[0:att, :])
              + _dot(r_ref[...].astype(BF16), wo_ref[att:, :]))
        x1_ref[...] = x1
        ms = jnp.mean(x1 * x1, axis=-1, keepdims=True)
        h2_ref[...] = (x1 * lax.rsqrt(ms + RMS_EPS) * g_ref[...]).astype(BF16)
        acc_ref[...] = jnp.zeros_like(acc_ref)

    h2 = h2_ref[...]
    gate = _dot(h2, wg_ref[...])
    up = _dot(h2, wu_ref[...])
    act = gate * jax.nn.sigmoid(gate) * up
    acc_ref[...] += _dot(act.astype(BF16), wd_ref[...])

    @pl.when(j == pl.num_programs(1) - 1)
    def _():
        o_ref[...] = x1_ref[...] + acc_ref[...]


def _out_ffn(x2d, attn, rwkv, wo_b, g, wfi_b, wfo_b, tm, hc):
    n, d = x2d.shape
    att = attn.shape[1]
    hid = wfo_b.shape[0]
    nh = hid // hc
    row = lambda i, j: (i, 0)
    const = lambda i, j: (0, 0)
    return pl.pallas_call(
        functools.partial(_out_ffn_kernel, att=att),
        grid=(n // tm, nh),
        in_specs=[
            pl.BlockSpec((tm, d), row),
            pl.BlockSpec((tm, att), row),
            pl.BlockSpec((tm, rwkv.shape[1]), row),
            pl.BlockSpec(wo_b.shape, const),
            pl.BlockSpec((1, d), const),
            pl.BlockSpec((d, hc), lambda i, j: (0, j)),
            pl.BlockSpec((d, hc), lambda i, j: (0, j + nh)),
            pl.BlockSpec((hc, d), lambda i, j: (j, 0)),
        ],
        out_specs=pl.BlockSpec((tm, d), row),
        out_shape=jax.ShapeDtypeStruct((n, d), F32),
        scratch_shapes=[pltpu.VMEM((tm, d), F32), pltpu.VMEM((tm, d), BF16), pltpu.VMEM((tm, d), F32)],
        compiler_params=pltpu.CompilerParams(
            dimension_semantics=("arbitrary", "arbitrary"), vmem_limit_bytes=VMEM_LIMIT),
        name="out_ffn",
    )(x2d, attn, rwkv, wo_b, g, wfi_b, wfi_b, wfo_b)


def _rms_kernel(x_ref, g_ref, o_ref):
    x = x_ref[...]
    ms = jnp.mean(x * x, axis=-1, keepdims=True)
    o_ref[...] = x * lax.rsqrt(ms + RMS_EPS) * g_ref[...]


def _rms_norm(x2d, g, tm):
    n, d = x2d.shape
    return pl.pallas_call(
        _rms_kernel,
        grid=(n // tm,),
        in_specs=[pl.BlockSpec((tm, d), lambda i: (i, 0)), pl.BlockSpec((1, d), lambda i: (0, 0))],
        out_specs=pl.BlockSpec((tm, d), lambda i: (i, 0)),
        out_shape=jax.ShapeDtypeStruct((n, d), F32),
        name="rms_norm",
    )(x2d, g)


def _ffn_chunk(hid):
    best = 128
    for hc in range(128, hid // 2 + 1, 128):
        if hid % hc == 0:
            best = hc
    return best


def kernel(x_prompt, x_sample, cache_k, cache_v, state_wkv, state_shift, page_table, norm_attn_g, w_in, shift_mu, decay_w0, decay_up, iclr_a0, iclr_up, gate_up, k_k, k_a, r_k, lnx_g, lnx_b, w_out, norm_ffn_g, w_ffn_in, w_ffn_out, norm_final_g):
    bsz, seq, d = x_prompt.shape
    db, ts, _ = x_sample.shape
    depth = w_in.shape[0]
    rwd = k_k.shape[1]
    att = (w_in.shape[2] - 3 * rwd - LORA_DECAY - LORA_A - LORA_GATE) // 3
    nheads_att = att // HEAD_DIM
    nheads_rw = rwd // HEAD_DIM
    cols = 3 * rwd + LORA_DECAY + LORA_A + LORA_GATE
    page = cache_k.shape[2]
    npages = page_table.shape[1]
    past_len = npages * page
    hid = w_ffn_out.shape[1]
    hc = _ffn_chunk(hid)

    np_tok = bsz * seq
    ns_tok = db * ts
    tm_p = 256
    tabs_p = _rope_tables(jnp.arange(seq, dtype=jnp.int32), att)
    pos_s = past_len + jnp.arange(ts, dtype=jnp.int32)
    tabs_s = tuple(jnp.tile(a, (db, 1)) for a in _rope_tables(pos_s, att))

    head_of = jnp.arange(rwd) // HEAD_DIM
    same_head = head_of[:, None] == head_of[None, :]
    ones_bd = same_head.astype(BF16)
    avg_bd = (same_head.astype(F32) / HEAD_DIM).astype(BF16)

    ts_pad = WKV_CHUNK
    zeros_shift = jnp.zeros((bsz, 1, cols), F32)
    zeros_state = jnp.zeros((bsz, nheads_rw // QUAD_HEADS, QUAD, QUAD), F32)

    xp = x_prompt.reshape(np_tok, d)
    xs = x_sample.reshape(ns_tok, d)
    outs = {n: [] for n in ("kp", "vp", "sp", "hp", "ks", "vs", "ss", "hs")}
    for l in range(depth):
        w_in_b = w_in[l].astype(BF16)
        wo_b = w_out[l].astype(BF16)
        wfi_b = w_ffn_in[l].astype(BF16)
        wfo_b = w_ffn_out[l].astype(BF16)
        g_attn = norm_attn_g[l].reshape(1, d)
        g_ffn = norm_ffn_g[l].reshape(1, d)
        prm = {
            "mu": shift_mu[l].reshape(1, cols),
            "w0": decay_w0[l].reshape(1, rwd),
            "wdec": jnp.concatenate([decay_up[l], jnp.zeros((LORA_A, rwd), F32)], axis=0),
            "a0": iclr_a0[l].reshape(1, rwd),
            "wa": jnp.concatenate([jnp.zeros((LORA_DECAY, rwd), F32), iclr_up[l]], axis=0),
            "wg": gate_up[l],
            "k_k": k_k[l].reshape(1, rwd),
            "k_a": k_a[l].reshape(1, rwd),
            "r_k": r_k[l].reshape(1, rwd),
            "lnx_g": lnx_g[l].reshape(1, rwd),
            "lnx_b": lnx_b[l].reshape(1, rwd),
            "ones": ones_bd,
            "avg": avg_bd,
        }

        q, k, v, rw = _norm_proj(xp, g_attn, w_in_b, tabs_p, tm_p, att)
        attn = _moba_prompt(q.reshape(bsz, seq, att), k.reshape(bsz, seq, att), v.reshape(bsz, seq, att))
        rw3 = rw.reshape(bsz, seq, cols)
        mix, hfin = _rwkv_mix(rw3, zeros_shift, zeros_state, prm, WKV_TILE)
        xp = _out_ffn(xp, attn.reshape(np_tok, att), mix.reshape(np_tok, rwd), wo_b, g_ffn, wfi_b, wfo_b, tm_p, hc)
        outs["kp"].append(k.reshape(bsz, seq, nheads_att, HEAD_DIM))
        outs["vp"].append(v.reshape(bsz, seq, nheads_att, HEAD_DIM))
        outs["sp"].append(_blockdiag_to_state(hfin))
        outs["hp"].append(rw3[:, -1])

        q, k, v, rw = _norm_proj(xs, g_attn, w_in_b, tabs_s, ns_tok, att)
        q3, k3, v3 = (a.reshape(db, ts, att) for a in (q, k, v))
        attn = _moba_sample(q3, k3, v3, cache_k, cache_v, page_table, l)
        rw3 = rw.reshape(db, ts, cols)
        rw_pad = jnp.pad(rw3, ((0, 0), (0, ts_pad - ts), (0, 0)))
        mix, hfin = _rwkv_mix(rw_pad, state_shift[l].reshape(db, 1, cols), _state_to_blockdiag(state_wkv[l]),
                              prm, ts_pad, t_valid=ts)
        mix = mix[:, :ts].reshape(ns_tok, rwd)
        xs = _out_ffn(xs, attn.reshape(ns_tok, att), mix, wo_b, g_ffn, wfi_b, wfo_b, ns_tok, hc)
        outs["ks"].append(k3.reshape(db, ts, nheads_att, HEAD_DIM))
        outs["vs"].append(v3.reshape(db, ts, nheads_att, HEAD_DIM))
        outs["ss"].append(_blockdiag_to_state(hfin))
        outs["hs"].append(rw3[:, -1])

    g_fin = norm_final_g.reshape(1, d)
    y_prompt = _rms_norm(xp, g_fin, tm_p).reshape(bsz, seq, d)
    y_sample = _rms_norm(xs, g_fin, ns_tok).reshape(db, ts, d)
    st = {n: jnp.stack(v) for n, v in outs.items()}
    return (y_prompt, y_sample, st["kp"], st["vp"], st["sp"], st["hp"], st["ks"], st["vs"], st["ss"], st["hs"])
```

```python
import functools

import jax
import jax.numpy as jnp
from jax import lax
from jax.experimental import pallas as pl
from jax.experimental.pallas import tpu as pltpu

F32 = jnp.float32
BF16 = jnp.bfloat16

HEAD_DIM = 64
MOBA_BLOCK = 256
MOBA_TOPK = 3
ROT_DIM = HEAD_DIM // 4
ROPE_THETA = 500000.0
LORA_DECAY = 64
LORA_A = 64
LORA_GATE = 128
RMS_EPS = 1e-6
GN_EPS = 64e-5
NEG_INF = -1e30

QUAD_HEADS = 4
QUAD = QUAD_HEADS * HEAD_DIM
WKV_CHUNK = 64
WKV_TILE = 256
SUBLANES = 8
DENSE_TILE = 512
VMEM_LIMIT = 56 * 1024 * 1024

_NN = (((1,), (0,)), ((), ()))
_NT = (((1,), (1,)), ((), ()))
_TN = (((0,), (0,)), ((), ()))


def _split(x, n):
    parts = []
    for i in range(n):
        p = x.astype(BF16)
        parts.append(p)
        if i + 1 < n:
            x = x - p.astype(F32)
    return parts


def _mdot(pa, pb, dims=_NN):
    order = max(len(pa), len(pb))
    acc = None
    for i in reversed(range(len(pa))):
        for j in reversed(range(len(pb))):
            if i + j < order:
                t = lax.dot_general(pa[i], pb[j], dims, preferred_element_type=F32)
                acc = t if acc is None else acc + t
    return acc


def _dot(a, b, dims=_NN):
    return lax.dot_general(a, b, dims, preferred_element_type=F32)


def _iota(shape, dim):
    return lax.broadcasted_iota(jnp.int32, shape, dim)


def _top3(gate, nvalid, axis):
    blk = _iota(gate.shape, axis)
    g = jnp.where(blk < nvalid, gate, NEG_INF)
    sels = []
    for _ in range(MOBA_TOPK):
        m = jnp.max(g, axis=axis, keepdims=True)
        idx = jnp.min(jnp.where(g == m, blk, jnp.int32(1 << 30)), axis=axis, keepdims=True)
        sels.append(jnp.where(m > 0.5 * NEG_INF, idx, -1))
        g = jnp.where(blk == idx, NEG_INF, g)
    return sels


def _norm_proj_kernel(x_ref, g_ref, w_ref, cos_ref, sna_ref, snb_ref, q_ref, k_ref, v_ref, rw_ref, *, att):
    x = x_ref[...]
    ms = jnp.mean(x * x, axis=-1, keepdims=True)
    h = (x * lax.rsqrt(ms + RMS_EPS) * g_ref[...]).astype(BF16)
    cos, sna, snb = cos_ref[...], sna_ref[...], snb_ref[...]
    half = ROT_DIM // 2

    def rope(p):
        return p * cos + pltpu.roll(p, att - half, 1) * sna + pltpu.roll(p, half, 1) * snb

    q_ref[...] = rope(_dot(h, w_ref[:, 0:att]))
    k_ref[...] = rope(_dot(h, w_ref[:, att:2 * att]))
    v_ref[...] = _dot(h, w_ref[:, 2 * att:3 * att])
    rw_ref[...] = _dot(h, w_ref[:, 3 * att:])


def _norm_proj(x2d, g, w_b, tabs, tm, att):
    n, d = x2d.shape
    cols = w_b.shape[1]
    rwc = cols - 3 * att
    ntab = tabs[0].shape[0] // tm
    row = lambda i: (i, 0)
    tab = lambda i: (i % ntab, 0)
    const = lambda i: (0, 0)
    return pl.pallas_call(
        functools.partial(_norm_proj_kernel, att=att),
        grid=(n // tm,),
        in_specs=[
            pl.BlockSpec((tm, d), row),
            pl.BlockSpec((1, d), const),
            pl.BlockSpec((d, cols), const),
            pl.BlockSpec((tm, att), tab),
            pl.BlockSpec((tm, att), tab),
            pl.BlockSpec((tm, att), tab),
        ],
        out_specs=[
            pl.BlockSpec((tm, att), row),
            pl.BlockSpec((tm, att), row),
            pl.BlockSpec((tm, att), row),
            pl.BlockSpec((tm, rwc), row),
        ],
        out_shape=[
            jax.ShapeDtypeStruct((n, att), F32),
            jax.ShapeDtypeStruct((n, att), F32),
            jax.ShapeDtypeStruct((n, att), F32),
            jax.ShapeDtypeStruct((n, rwc), F32),
        ],
        compiler_params=pltpu.CompilerParams(
            dimension_semantics=("arbitrary",), vmem_limit_bytes=VMEM_LIMIT),
        name="norm_proj",
    )(x2d, g, w_b, *tabs)


def _rope_tables(pos, att):
    half = ROT_DIM // 2
    inv = jnp.float32(ROPE_THETA) ** (-jnp.arange(half, dtype=F32) / half)
    ang = pos.astype(F32)[:, None] * inv[None, :]
    cos, sin = jnp.cos(ang), jnp.sin(ang)
    t = pos.shape[0]
    rest = HEAD_DIM - ROT_DIM
    cos_h = jnp.concatenate([cos, cos, jnp.ones((t, rest), F32)], axis=1)
    sna_h = jnp.concatenate([-sin, jnp.zeros((t, half + rest), F32)], axis=1)
    snb_h = jnp.concatenate([jnp.zeros((t, half), F32), sin, jnp.zeros((t, rest), F32)], axis=1)
    nh = att // HEAD_DIM
    return tuple(jnp.tile(a, (1, nh)) for a in (cos_h, sna_h, snb_h))


def _moba_prompt_kernel(qt_ref, k_ref, vt_ref, o_ref, kb_ref, kmean_ref, *, nb):
    i = pl.program_id(2)
    blk = MOBA_BLOCK

    @pl.when(i == 0)
    def _():
        for j in range(nb):
            kj = k_ref[0, j * blk:(j + 1) * blk, :]
            kmean_ref[j:j + 1, :] = jnp.mean(kj, axis=0, keepdims=True)
            kb_ref[j * blk:(j + 1) * blk, :] = kj.astype(BF16)

    qt = qt_ref[0, 0] * (HEAD_DIM ** -0.5)
    row_head = _iota((QUAD, 1), 0) // HEAD_DIM
    causal = _iota((blk, blk), 0) <= _iota((blk, blk), 1)
    kmean = _split(kmean_ref[...], 2)
    k_own = kb_ref[pl.ds(pl.multiple_of(i * blk, blk), blk), :]

    head_rows = [slice(h * HEAD_DIM, (h + 1) * HEAD_DIM) for h in range(QUAD_HEADS)]
    qms = [jnp.where(row_head == h, qt, 0.0) for h in range(QUAD_HEADS)]
    qmbs = [qm.astype(BF16) for qm in qms]
    gates = [_mdot(kmean, _split(qm, 2)) for qm in qms]
    own_scores = [_dot(k_own, qmb) for qmb in qmbs]
    sels = [_top3(g, i, axis=0) for g in gates]
    state = []
    for h in range(QUAD_HEADS):
        s = jnp.where(causal, own_scores[h], NEG_INF)
        m = jnp.max(s, axis=0, keepdims=True)
        p = jnp.exp(s - m)
        l = jnp.sum(p, axis=0, keepdims=True)
        state += [m, l, _dot(vt_ref[0, i, head_rows[h], :], p.astype(BF16))]

    def body(j, carry):
        kj = kb_ref[pl.ds(pl.multiple_of(j * blk, blk), blk), :]
        scores = [_dot(kj, qmbs[h]) for h in range(QUAD_HEADS)]
        new = []
        for h in range(QUAD_HEADS):
            m, l, acc = carry[3 * h:3 * h + 3]
            s1, s2, s3 = sels[h]
            picked = (s1 == j) | (s2 == j) | (s3 == j)
            s = jnp.where(picked, scores[h], NEG_INF)
            m_new = jnp.maximum(m, jnp.max(s, axis=0, keepdims=True))
            alpha = jnp.exp(m - m_new)
            p = jnp.exp(s - m_new)
            l = alpha * l + jnp.sum(p, axis=0, keepdims=True)
            acc = acc * alpha + _dot(vt_ref[0, j, head_rows[h], :], p.astype(BF16))
            new += [m_new, l, acc]
        return tuple(new)

    state = lax.fori_loop(0, i, body, tuple(state))
    o_ref[0, 0] = jnp.concatenate([state[3 * h + 2] / state[3 * h + 1] for h in range(QUAD_HEADS)], axis=0)


def _moba_prompt(q, k, v):
    b, t, att = q.shape
    nb = t // MOBA_BLOCK
    nq = att // QUAD
    blk = MOBA_BLOCK
    qt = q.reshape(b, nb, blk, att).transpose(0, 1, 3, 2)
    vt = v.astype(BF16).reshape(b, nb, blk, att).transpose(0, 1, 3, 2)
    ot = pl.pallas_call(
        functools.partial(_moba_prompt_kernel, nb=nb),
        grid=(b, nq, nb),
        in_specs=[
            pl.BlockSpec((1, 1, QUAD, blk), lambda bi, qi, i: (bi, i, qi, 0)),
            pl.BlockSpec((1, t, QUAD), lambda bi, qi, i: (bi, 0, qi)),
            pl.BlockSpec((1, nb, QUAD, blk), lambda bi, qi, i: (bi, 0, qi, 0)),
        ],
        out_specs=pl.BlockSpec((1, 1, QUAD, blk), lambda bi, qi, i: (bi, i, qi, 0)),
        out_shape=jax.ShapeDtypeStruct((b, nb, att, blk), F32),
        scratch_shapes=[
            pltpu.VMEM((t, QUAD), BF16),
            pltpu.VMEM((nb, QUAD), F32),
        ],
        compiler_params=pltpu.CompilerParams(
            dimension_semantics=("arbitrary", "arbitrary", "arbitrary"), vmem_limit_bytes=VMEM_LIMIT),
        name="moba_prompt",
    )(qt, k, vt)
    return ot.transpose(0, 1, 3, 2).reshape(b, t, att)


PAGES_PER_STEP = 16


def _moba_sample_kernel(pt_ref, qh_ref, qt_ref, kn_ref, vn_ref, *refs, ts, nh, npages, page):
    g_pages = PAGES_PER_STEP
    k_refs = refs[:g_pages]
    v_refs = refs[g_pages:2 * g_pages]
    o_ref = refs[2 * g_pages]
    qb_ref, s_ref, sel_ref, m_ref, l_ref, own_ref, acc_ref = refs[2 * g_pages + 1:]
    ph = pl.program_id(1)
    p = pl.program_id(2)
    nsteps = npages // g_pages
    rows = nh * ts
    ppb = MOBA_BLOCK // page
    nblk = npages // ppb
    scale = HEAD_DIM ** -0.5
    groups = HEAD_DIM // SUBLANES

    @pl.when((ph == 0) & (p == 0))
    def _():
        qt = qt_ref[0] * scale
        for r in range(rows):
            qb_ref[r] = jnp.broadcast_to(qt[:, r:r + 1], (HEAD_DIM, page))
        acc_ref[...] = jnp.zeros_like(acc_ref)

    @pl.when(ph == 0)
    def _():
        fold = (_iota((rows, rows * SUBLANES), 1) // SUBLANES == _iota((rows, rows * SUBLANES), 0)).astype(BF16)
        for g in range(g_pages):
            parts = []
            for h in range(nh):
                kt = k_refs[g][0, 0, h]
                for t in range(ts):
                    prod = kt * qb_ref[h * ts + t]
                    part = prod[0:SUBLANES]
                    for u in range(1, groups):
                        part = part + prod[u * SUBLANES:(u + 1) * SUBLANES]
                    parts.append(part)
            stacked = jnp.concatenate(parts, axis=0)
            s_ref[p * g_pages + g] = _mdot([fold], _split(stacked, 3))

    @pl.when((ph == 0) & (p == nsteps - 1))
    def _():
        lane = _iota((rows, page), 1)

        def gate_sum(pg, gate):
            rs = jnp.sum(s_ref[pg], axis=1, keepdims=True)
            return gate + jnp.where(lane == pg // ppb, rs, 0.0)

        gate = lax.fori_loop(0, npages, gate_sum, jnp.zeros((rows, page), F32))
        sels = _top3(gate, nblk, axis=1)
        for r in range(MOBA_TOPK):
            sel_ref[r] = sels[r]

        def picked(pg):
            b_ = pg // ppb
            return (sels[0] == b_) | (sels[1] == b_) | (sels[2] == b_)

        def mx(pg, m):
            return jnp.maximum(m, jnp.max(jnp.where(picked(pg), s_ref[pg], NEG_INF), axis=1, keepdims=True))

        m = lax.fori_loop(0, npages, mx, jnp.full((rows, 1), NEG_INF, F32))
        qh = qh_ref[0] * scale
        row_t = _iota((rows, 1), 0) % ts
        s_own = []
        for t in range(ts):
            st = jnp.sum(qh * kn_ref[0, t], axis=1, keepdims=True)
            st = jnp.where(row_t >= t, st, NEG_INF)
            s_own.append(st)
            m = jnp.maximum(m, st)

        def weights(pg, l):
            pe = jnp.where(picked(pg), jnp.exp(s_ref[pg] - m), 0.0)
            s_ref[pg] = pe
            return l + jnp.sum(pe, axis=1, keepdims=True)

        l = lax.fori_loop(0, npages, weights, jnp.zeros((rows, 1), F32))
        own = jnp.zeros((rows, HEAD_DIM), F32)
        for t in range(ts):
            pt_ = jnp.exp(s_own[t] - m)
            l = l + pt_
            own = own + pt_ * vn_ref[0, t]
        l_ref[...] = l
        own_ref[...] = own

    @pl.when(ph == 1)
    def _():
        for h in range(nh):
            parts = [None] * ts
            for g in range(g_pages):
                vt = v_refs[g][0, 0, h]
                for t in range(ts):
                    r = h * ts + t
                    term = vt * s_ref[p * g_pages + g, r:r + 1, :]
                    parts[t] = term if parts[t] is None else parts[t] + term
            for t in range(ts):
                acc_ref[h * ts + t] += parts[t]

    @pl.when((ph == 1) & (p == nsteps - 1))
    def _():
        lane = _iota((HEAD_DIM, page), 1)
        cols = jnp.zeros((HEAD_DIM, page), F32)
        for r in range(rows):
            cols = jnp.where(lane == r, jnp.sum(acc_ref[r], axis=1, keepdims=True), cols)
        past = jnp.transpose(cols)[0:rows, :]
        o_ref[0] = (past + own_ref[...]) / l_ref[...]


def _moba_sample(q, k_new, v_new, cache_k, cache_v, page_table, layer):
    db, ts, att = q.shape
    npages = page_table.shape[1]
    page = cache_k.shape[2]
    g_pages = PAGES_PER_STEP
    nsteps = npages // g_pages
    nh = att // HEAD_DIM
    rows = ts * nh
    assert npages % g_pages == 0 and MOBA_BLOCK % page == 0 and rows <= page
    assert npages * page // MOBA_BLOCK <= page

    to_rows = lambda a: a.reshape(db, ts, nh, HEAD_DIM).transpose(0, 2, 1, 3).reshape(db, rows, HEAD_DIM)
    qh = to_rows(q)
    qt = qh.transpose(0, 2, 1)
    rep = lambda a: jnp.broadcast_to(a.reshape(db, ts, nh, 1, HEAD_DIM),
                                     (db, ts, nh, ts, HEAD_DIM)).reshape(db, ts, rows, HEAD_DIM)
    kn = rep(k_new)
    vn = rep(v_new)
    ckt = cache_k.transpose(0, 1, 3, 4, 2)
    cvt = cache_v.transpose(0, 1, 3, 4, 2)

    def k_map(g):
        def f(b, ph, p, pt):
            pg = jnp.where(ph == 0, p, nsteps - 1) * g_pages + g
            return (layer, pt[b, pg], 0, 0, 0)
        return f

    def v_map(g):
        def f(b, ph, p, pt):
            pg = jnp.where(ph == 0, 0, p) * g_pages + g
            return (layer, pt[b, pg], 0, 0, 0)
        return f

    tok3 = lambda b, ph, p, pt: (b, 0, 0)
    tok4 = lambda b, ph, p, pt: (b, 0, 0, 0)
    grid_spec = pltpu.PrefetchScalarGridSpec(
        num_scalar_prefetch=1,
        grid=(db, 2, nsteps),
        in_specs=(
            [pl.BlockSpec((1, rows, HEAD_DIM), tok3), pl.BlockSpec((1, HEAD_DIM, rows), tok3),
             pl.BlockSpec((1, ts, rows, HEAD_DIM), tok4), pl.BlockSpec((1, ts, rows, HEAD_DIM), tok4)]
            + [pl.BlockSpec((1, 1, nh, HEAD_DIM, page), k_map(g)) for g in range(g_pages)]
            + [pl.BlockSpec((1, 1, nh, HEAD_DIM, page), v_map(g)) for g in range(g_pages)]
        ),
        out_specs=pl.BlockSpec((1, rows, HEAD_DIM), tok3),
        scratch_shapes=[
            pltpu.VMEM((rows, HEAD_DIM, page), F32),
            pltpu.VMEM((npages, rows, page), F32),
            pltpu.VMEM((MOBA_TOPK, rows, 1), jnp.int32),
            pltpu.VMEM((rows, 1), F32),
            pltpu.VMEM((rows, 1), F32),
            pltpu.VMEM((rows, HEAD_DIM), F32),
            pltpu.VMEM((rows, HEAD_DIM, page), F32),
        ],
    )
    o = pl.pallas_call(
        functools.partial(_moba_sample_kernel, ts=ts, nh=nh, npages=npages, page=page),
        grid_spec=grid_spec,
        out_shape=jax.ShapeDtypeStruct((db, rows, HEAD_DIM), F32),
        compiler_params=pltpu.CompilerParams(
            dimension_semantics=("arbitrary", "arbitrary", "arbitrary"), vmem_limit_bytes=VMEM_LIMIT),
        name="moba_sample",
    )(page_table, qh, qt, kn, vn, *([ckt] * g_pages), *([cvt] * g_pages))
    return o.reshape(db, nh, ts, HEAD_DIM).transpose(0, 2, 1, 3).reshape(db, ts, att)


def _wkv_local(insts, cst):
    c = WKV_CHUNK
    tri, strict, incl, eye_cat, bdmask, hmasks, hbd, eye_q = cst

    def each(f, *lists):
        return [f(*a) for a in zip(*lists)]

    def stack(x):
        xb = x.astype(BF16)
        return jnp.concatenate([jnp.where(hm, xb, jnp.zeros_like(xb)) for hm in hmasks], axis=0)

    def bdiag(x):
        xb = x.astype(BF16)
        return jnp.where(bdmask, jnp.concatenate([xb] * QUAD_HEADS, axis=0), jnp.zeros((QUAD, QUAD), BF16))

    rs, lws, kks, bbs, ks, vs = (list(x) for x in zip(*insts))
    cums = each(lambda lw: _mdot([tri], _split(lw, 3)), lws)
    cls = [cum[c - 1:c, :] for cum in cums]
    kkts = each(lambda kk, cum, lw: kk * jnp.exp(cum - lw), kks, cums, lws)
    rts = each(lambda r, cum: r * jnp.exp(cum), rs, cums)
    e_invs = [jnp.exp(-cum) for cum in cums]
    khs = each(lambda k, e: k * e, ks, e_invs)
    bhs = each(lambda b, e: b * e, bbs, e_invs)
    e_rems = each(lambda cl, cum: jnp.exp(cl - cum), cls, cums)
    kgs = each(lambda k, e: k * e, ks, e_rems)
    bgs = each(lambda b, e: (b * e).astype(BF16), bbs, e_rems)

    lhss = each(lambda kkt, rt: jnp.concatenate([kkt, rt], axis=0).astype(BF16), kkts, rts)
    vsts = [stack(v) for v in vs]
    a_ks = each(lambda lhs, kh: _dot(lhs, stack(kh), _NT), lhss, khs)
    a_bs = each(lambda lhs, bh: _dot(lhs, stack(bh), _NT), lhss, bhs)
    l_ks = [jnp.where(strict, a[:c], 0.0).astype(BF16) for a in a_ks]
    n_bs = [jnp.where(strict, -a[:c], 0.0) for a in a_bs]
    a_rks = [jnp.where(incl, a[c:], 0.0).astype(BF16) for a in a_ks]
    a_rbs = [jnp.where(incl, a[c:], 0.0).astype(BF16) for a in a_bs]

    t_invs = [eye_cat + n for n in n_bs]
    pws = n_bs
    for _ in range(c.bit_length() - 2):
        pws = each(lambda pw: _dot(pw.astype(BF16), bdiag(pw)), pws)
        t_invs = each(lambda t, pw: t + _dot(t.astype(BF16), bdiag(pw)), t_invs, pws)
    t_bs = [t.astype(BF16) for t in t_invs]

    lkvs = each(_dot, l_ks, vsts)
    uks = each(lambda t, kkt: _dot(t, stack(kkt)), t_bs, kkts)
    ucs = each(lambda t, lkv: _dot(t, stack(lkv)), t_bs, lkvs)
    rys = each(lambda rt, a_rb, uk: rt - _dot(a_rb, stack(uk)), rts, a_rbs, uks)
    ycs = each(lambda a_rk, vst, a_rb, uc: _dot(a_rk, vst) - _dot(a_rb, stack(uc)), a_rks, vsts, a_rbs, ucs)
    gcols = [jnp.transpose(jnp.broadcast_to(jnp.exp(cl), (SUBLANES, QUAD)))[:, 0:1] for cl in cls]
    m_mats = each(lambda gcol, bg, uk: jnp.where(eye_q, gcol, 0.0)
                  - jnp.where(hbd, _dot(bg, uk.astype(BF16), _TN), 0.0), gcols, bgs, uks)
    n_mats = each(lambda kg, bg, v, uc: jnp.where(
        hbd, _dot(jnp.concatenate([kg.astype(BF16), -bg], axis=0),
                  jnp.concatenate([v, uc], axis=0).astype(BF16), _TN), 0.0), kgs, bgs, vs, ucs)
    return list(zip(m_mats, n_mats, rys, ycs))


def _rwkv_kernel(rw_ref, sh_ref, h0_ref, mu_ref, w0_ref, wdec_ref, a0_ref, wa_ref, wg_ref, kk_ref, ka_ref,
                 rk_ref, lg_ref, lb_ref, ones_ref, avg_ref, out_ref, hout_ref,
                 h_ref, carry_ref, *, tt, t_valid, rwd):
    t = pl.program_id(1)
    c = WKV_CHUNK
    nq = rwd // QUAD
    nch = tt // c

    @pl.when(t == 0)
    def _():
        h_ref[...] = h0_ref[0]
        carry_ref[...] = sh_ref[0]

    rwf = rw_ref[0]
    row = _iota((tt, 1), 0)
    prev = jnp.where(row == 0, carry_ref[...], pltpu.roll(rwf, 1, 0))
    carry_ref[...] = rwf[tt - 1:tt, :]
    xs = rwf + mu_ref[...] * (prev - rwf)
    r = xs[:, 0:rwd]
    kr = xs[:, rwd:2 * rwd]
    vr = xs[:, 2 * rwd:3 * rwd]
    la = xs[:, 3 * rwd:3 * rwd + LORA_DECAY + LORA_A]
    gd = xs[:, 3 * rwd + LORA_DECAY + LORA_A:]
    w = w0_ref[...] + _mdot(_split(jnp.tanh(la), 2), _split(wdec_ref[...], 2))
    lw = -jnp.exp(-jax.nn.softplus(-w) - 0.5)
    a = jax.nn.sigmoid(a0_ref[...] + _mdot(_split(la, 2), _split(wa_ref[...], 2)))
    g = _mdot(_split(jax.nn.sigmoid(gd), 2), _split(wg_ref[...], 2))
    kk = kr * kk_ref[...]
    km = kr * (1.0 + (a - 1.0) * ka_ref[...])
    ones = ones_ref[...]
    ss = _mdot(_split(kk * kk, 3), [ones])
    kkn = kk / jnp.maximum(jnp.sqrt(ss), 1e-12)
    bb = kkn * a
    if t_valid is not None:
        valid = (t * tt + row) < t_valid
        lw = jnp.where(valid, lw, 0.0)
        kkn = jnp.where(valid, kkn, 0.0)
        bb = jnp.where(valid, bb, 0.0)
        km = jnp.where(valid, km, 0.0)
        vr = jnp.where(valid, vr, 0.0)

    lane_c = _iota((c, QUAD_HEADS * c), 1) % c
    row_c = _iota((c, QUAD_HEADS * c), 0)
    lane_head = _iota((1, QUAD), 1) // HEAD_DIM
    cst = (
        (_iota((c, c), 1) <= _iota((c, c), 0)).astype(BF16),
        lane_c < row_c,
        lane_c <= row_c,
        (lane_c == row_c).astype(F32),
        (_iota((QUAD_HEADS * c, QUAD_HEADS * c), 0) // c) == (_iota((QUAD_HEADS * c, QUAD_HEADS * c), 1) // c),
        [lane_head == h for h in range(QUAD_HEADS)],
        (_iota((QUAD, QUAD), 0) // HEAD_DIM) == (_iota((QUAD, QUAD), 1) // HEAD_DIM),
        _iota((QUAD, QUAD), 0) == _iota((QUAD, QUAD), 1),
    )
    insts = []
    for ci in range(nch):
        rs = slice(ci * c, (ci + 1) * c)
        for qd in range(nq):
            sl = slice(qd * QUAD, (qd + 1) * QUAD)
            insts.append((r[rs, sl], lw[rs, sl], kkn[rs, sl], bb[rs, sl], km[rs, sl], vr[rs, sl]))
    local = _wkv_local(insts, cst)
    hst = [h_ref[qd] for qd in range(nq)]
    ys = [[] for _ in range(nq)]
    for ci in range(nch):
        hb = [h.astype(BF16) for h in hst]
        for qd in range(nq):
            _, _, ry, yc = local[ci * nq + qd]
            ys[qd].append(_dot(ry.astype(BF16), hb[qd]) + yc)
        hst = [_dot(local[ci * nq + qd][0].astype(BF16), hb[qd]) + local[ci * nq + qd][1] for qd in range(nq)]
    for qd in range(nq):
        h_ref[qd] = hst[qd]
    y = jnp.concatenate([jnp.concatenate(col, axis=0) if nch > 1 else col[0] for col in ys], axis=1)
    hout_ref[0] = h_ref[...]

    avg = avg_ref[...]
    mean = _mdot(_split(y, 3), [avg])
    d = y - mean
    var = _mdot(_split(d * d, 3), [avg])
    yn = d * lax.rsqrt(var + GN_EPS) * lg_ref[...] + lb_ref[...]
    bonus = _mdot(_split(r * km * rk_ref[...], 3), [ones]) * vr
    out_ref[0] = (yn + bonus) * g


def _rwkv_mix(rw, shift0, h0, prm, tt, t_valid=None):
    b, t, cols = rw.shape
    rwd = prm["k_k"].shape[1]
    nq = rwd // QUAD
    vec = lambda n: pl.BlockSpec((1, n), lambda bi, ti: (0, 0))
    mat = lambda m, n: pl.BlockSpec((m, n), lambda bi, ti: (0, 0))
    lora_in = LORA_DECAY + LORA_A
    return pl.pallas_call(
        functools.partial(_rwkv_kernel, tt=tt, t_valid=t_valid, rwd=rwd),
        grid=(b, t // tt),
        in_specs=[
            pl.BlockSpec((1, tt, cols), lambda bi, ti: (bi, ti, 0)),
            pl.BlockSpec((1, 1, cols), lambda bi, ti: (bi, 0, 0)),
            pl.BlockSpec((1, nq, QUAD, QUAD), lambda bi, ti: (bi, 0, 0, 0)),
            vec(cols), vec(rwd), mat(lora_in, rwd), vec(rwd), mat(lora_in, rwd), mat(LORA_GATE, rwd),
            vec(rwd), vec(rwd), vec(rwd), vec(rwd), vec(rwd), mat(rwd, rwd), mat(rwd, rwd),
        ],
        out_specs=[
            pl.BlockSpec((1, tt, rwd), lambda bi, ti: (bi, ti, 0)),
            pl.BlockSpec((1, nq, QUAD, QUAD), lambda bi, ti: (bi, 0, 0, 0)),
        ],
        out_shape=[
            jax.ShapeDtypeStruct((b, t, rwd), F32),
            jax.ShapeDtypeStruct((b, nq, QUAD, QUAD), F32),
        ],
        scratch_shapes=[pltpu.VMEM((nq, QUAD, QUAD), F32), pltpu.VMEM((1, cols), F32)],
        compiler_params=pltpu.CompilerParams(
            dimension_semantics=("arbitrary", "arbitrary"), vmem_limit_bytes=VMEM_LIMIT),
        name="rwkv_mix",
    )(rw, shift0, h0, prm["mu"], prm["w0"], prm["wdec"], prm["a0"], prm["wa"], prm["wg"], prm["k_k"],
      prm["k_a"], prm["r_k"], prm["lnx_g"], prm["lnx_b"], prm["ones"], prm["avg"])


def _state_to_blockdiag(s):
    b, h, dv, dk = s.shape
    st = jnp.swapaxes(s, -1, -2).reshape(b, h // QUAD_HEADS, QUAD_HEADS, dk, dv)
    eye = jnp.eye(QUAD_HEADS, dtype=s.dtype)
    return jnp.einsum("bqhkv,hg->bqhkgv", st, eye).reshape(b, h // QUAD_HEADS, QUAD, QUAD)


def _blockdiag_to_state(hb):
    b, nq = hb.shape[:2]
    hr = hb.reshape(b, nq, QUAD_HEADS, HEAD_DIM, QUAD_HEADS, HEAD_DIM)
    st = jnp.einsum("bqhkhv->bqhkv", hr).reshape(b, nq * QUAD_HEADS, HEAD_DIM, HEAD_DIM)
    return jnp.swapaxes(st, -1, -2)


def _out_ffn_kernel(x_ref, a_ref, r_ref, wo_ref, g_ref, wg_ref, wu_ref, wd_ref, o_ref, x1_ref, h2_ref, acc_ref,
                    *, att):
    j = pl.program_id(1)

    @pl.when(j == 0)
    def _():
        x1 = (x_ref[...] + _dot(a_ref[...].astype(BF16), wo_ref[0:att, :])
              + _dot(r_ref[...].astype(BF16), wo_ref[att:, :]))
        x1_ref[...] = x1
        ms = jnp.mean(x1 * x1, axis=-1, keepdims=True)
        h2_ref[...] = (x1 * lax.rsqrt(ms + RMS_EPS) * g_ref[...]).astype(BF16)
        acc_ref[...] = jnp.zeros_like(acc_ref)

    h2 = h2_ref[...]
    gate = _dot(h2, wg_ref[...])
    up = _dot(h2, wu_ref[...])
    act = gate * jax.nn.sigmoid(gate) * up
    acc_ref[...] += _dot(act.astype(BF16), wd_ref[...])

    @pl.when(j == pl.num_programs(1) - 1)
    def _():
        o_ref[...] = x1_ref[...] + acc_ref[...]


def _out_ffn(x2d, attn, rwkv, wo_b, g, wfi_b, wfo_b, tm, hc):
    n, d = x2d.shape
    att = attn.shape[1]
    hid = wfo_b.shape[0]
    nh = hid // hc
    row = lambda i, j: (i, 0)
    const = lambda i, j: (0, 0)
    return pl.pallas_call(
        functools.partial(_out_ffn_kernel, att=att),
        grid=(n // tm, nh),
        in_specs=[
            pl.BlockSpec((tm, d), row),
            pl.BlockSpec((tm, att), row),
            pl.BlockSpec((tm, rwkv.shape[1]), row),
            pl.BlockSpec(wo_b.shape, const),
            pl.BlockSpec((1, d), const),
            pl.BlockSpec((d, hc), lambda i, j: (0, j)),
            pl.BlockSpec((d, hc), lambda i, j: (0, j + nh)),
            pl.BlockSpec((hc, d), lambda i, j: (j, 0)),
        ],
        out_specs=pl.BlockSpec((tm, d), row),
        out_shape=jax.ShapeDtypeStruct((n, d), F32),
        scratch_shapes=[pltpu.VMEM((tm, d), F32), pltpu.VMEM((tm, d), BF16), pltpu.VMEM((tm, d), F32)],
        compiler_params=pltpu.CompilerParams(
            dimension_semantics=("arbitrary", "arbitrary"), vmem_limit_bytes=VMEM_LIMIT),
        name="out_ffn",
    )(x2d, attn, rwkv, wo_b, g, wfi_b, wfi_b, wfo_b)


def _rms_kernel(x_ref, g_ref, o_ref):
    x = x_ref[...]
    ms = jnp.mean(x * x, axis=-1, keepdims=True)
    o_ref[...] = x * lax.rsqrt(ms + RMS_EPS) * g_ref[...]


def _rms_norm(x2d, g, tm):
    n, d = x2d.shape
    return pl.pallas_call(
        _rms_kernel,
        grid=(n // tm,),
        in_specs=[pl.BlockSpec((tm, d), lambda i: (i, 0)), pl.BlockSpec((1, d), lambda i: (0, 0))],
        out_specs=pl.BlockSpec((tm, d), lambda i: (i, 0)),
        out_shape=jax.ShapeDtypeStruct((n, d), F32),
        name="rms_norm",
    )(x2d, g)


def _ffn_chunk(hid):
    best = 128
    for hc in range(128, hid // 2 + 1, 128):
        if hid % hc == 0:
            best = hc
    return best


def kernel(x_prompt, x_sample, cache_k, cache_v, state_wkv, state_shift, page_table, norm_attn_g, w_in, shift_mu, decay_w0, decay_up, iclr_a0, iclr_up, gate_up, k_k, k_a, r_k, lnx_g, lnx_b, w_out, norm_ffn_g, w_ffn_in, w_ffn_out, norm_final_g):
    bsz, seq, d = x_prompt.shape
    db, ts, _ = x_sample.shape
    depth = w_in.shape[0]
    rwd = k_k.shape[1]
    att = (w_in.shape[2] - 3 * rwd - LORA_DECAY - LORA_A - LORA_GATE) // 3
    nheads_att = att // HEAD_DIM
    nheads_rw = rwd // HEAD_DIM
    cols = 3 * rwd + LORA_DECAY + LORA_A + LORA_GATE
    page = cache_k.shape[2]
    npages = page_table.shape[1]
    past_len = npages * page
    hid = w_ffn_out.shape[1]
    hc = _ffn_chunk(hid)

    np_tok = bsz * seq
    ns_tok = db * ts
    tm_p = DENSE_TILE
    tabs_p = _rope_tables(jnp.arange(seq, dtype=jnp.int32), att)
    pos_s = past_len + jnp.arange(ts, dtype=jnp.int32)
    tabs_s = tuple(jnp.tile(a, (db, 1)) for a in _rope_tables(pos_s, att))

    head_of = jnp.arange(rwd) // HEAD_DIM
    same_head = head_of[:, None] == head_of[None, :]
    ones_bd = same_head.astype(BF16)
    avg_bd = (same_head.astype(F32) / HEAD_DIM).astype(BF16)

    ts_pad = WKV_CHUNK
    zeros_shift = jnp.zeros((bsz, 1, cols), F32)
    zeros_state = jnp.zeros((bsz, nheads_rw // QUAD_HEADS, QUAD, QUAD), F32)

    xp = x_prompt.reshape(np_tok, d)
    xs = x_sample.reshape(ns_tok, d)
    outs = {n: [] for n in ("kp", "vp", "sp", "hp", "ks", "vs", "ss", "hs")}
    for l in range(depth):
        w_in_b = w_in[l].astype(BF16)
        wo_b = w_out[l].astype(BF16)
        wfi_b = w_ffn_in[l].astype(BF16)
        wfo_b = w_ffn_out[l].astype(BF16)
        g_attn = norm_attn_g[l].reshape(1, d)
        g_ffn = norm_ffn_g[l].reshape(1, d)
        prm = {
            "mu": shift_mu[l].reshape(1, cols),
            "w0": decay_w0[l].reshape(1, rwd),
            "wdec": jnp.concatenate([decay_up[l], jnp.zeros((LORA_A, rwd), F32)], axis=0),
            "a0": iclr_a0[l].reshape(1, rwd),
            "wa": jnp.concatenate([jnp.zeros((LORA_DECAY, rwd), F32), iclr_up[l]], axis=0),
            "wg": gate_up[l],
            "k_k": k_k[l].reshape(1, rwd),
            "k_a": k_a[l].reshape(1, rwd),
            "r_k": r_k[l].reshape(1, rwd),
            "lnx_g": lnx_g[l].reshape(1, rwd),
            "lnx_b": lnx_b[l].reshape(1, rwd),
            "ones": ones_bd,
            "avg": avg_bd,
        }

        q, k, v, rw = _norm_proj(xp, g_attn, w_in_b, tabs_p, tm_p, att)
        attn = _moba_prompt(q.reshape(bsz, seq, att), k.reshape(bsz, seq, att), v.reshape(bsz, seq, att))
        rw3 = rw.reshape(bsz, seq, cols)
        mix, hfin = _rwkv_mix(rw3, zeros_shift, zeros_state, prm, WKV_TILE)
        xp = _out_ffn(xp, attn.reshape(np_tok, att), mix.reshape(np_tok, rwd), wo_b, g_ffn, wfi_b, wfo_b, tm_p, hc)
        outs["kp"].append(k.reshape(bsz, seq, nheads_att, HEAD_DIM))
        outs["vp"].append(v.reshape(bsz, seq, nheads_att, HEAD_DIM))
        outs["sp"].append(_blockdiag_to_state(hfin))
        outs["hp"].append(rw3[:, -1])

        q, k, v, rw = _norm_proj(xs, g_attn, w_in_b, tabs_s, ns_tok, att)
        q3, k3, v3 = (a.reshape(db, ts, att) for a in (q, k, v))
        attn = _moba_sample(q3, k3, v3, cache_k, cache_v, page_table, l)
        rw3 = rw.reshape(db, ts, cols)
        rw_pad = jnp.pad(rw3, ((0, 0), (0, ts_pad - ts), (0, 0)))
        mix, hfin = _rwkv_mix(rw_pad, state_shift[l].reshape(db, 1, cols), _state_to_blockdiag(state_wkv[l]),
                              prm, ts_pad, t_valid=ts)
        mix = mix[:, :ts].reshape(ns_tok, rwd)
        xs = _out_ffn(xs, attn.reshape(ns_tok, att), mix, wo_b, g_ffn, wfi_b, wfo_b, ns_tok, hc)
        outs["ks"].append(k3.reshape(db, ts, nheads_att, HEAD_DIM))
        outs["vs"].append(v3.reshape(db, ts, nheads_att, HEAD_DIM))
        outs["ss"].append(_blockdiag_to_state(hfin))
        outs["hs"].append(rw3[:, -1])

    g_fin = norm_final_g.reshape(1, d)
    y_prompt = _rms_norm(xp, g_fin, tm_p).reshape(bsz, seq, d)
    y_sample = _rms_norm(xs, g_fin, ns_tok).reshape(db, ts, d)
    st = {n: jnp.stack(v) for n, v in outs.items()}
    return (y_prompt, y_sample, st["kp"], st["vp"], st["sp"], st["hp"], st["ks"], st["vs"], st["ss"], st["hs"])
```

```python
import functools

import jax
import jax.numpy as jnp
from jax import lax
from jax.experimental import pallas as pl
from jax.experimental.pallas import tpu as pltpu

F32 = jnp.float32
BF16 = jnp.bfloat16

HEAD_DIM = 64
MOBA_BLOCK = 256
MOBA_TOPK = 3
ROT_DIM = HEAD_DIM // 4
ROPE_THETA = 500000.0
LORA_DECAY = 64
LORA_A = 64
LORA_GATE = 128
RMS_EPS = 1e-6
GN_EPS = 64e-5
NEG_INF = -1e30

QUAD_HEADS = 4
QUAD = QUAD_HEADS * HEAD_DIM
WKV_CHUNK = 64
WKV_TILE = 256
SUBLANES = 8
DENSE_TILE = 512
VMEM_LIMIT = 56 * 1024 * 1024

_NN = (((1,), (0,)), ((), ()))
_NT = (((1,), (1,)), ((), ()))
_TN = (((0,), (0,)), ((), ()))


def _split(x, n):
    parts = []
    for i in range(n):
        p = x.astype(BF16)
        parts.append(p)
        if i + 1 < n:
            x = x - p.astype(F32)
    return parts


def _mdot(pa, pb, dims=_NN):
    order = max(len(pa), len(pb))
    acc = None
    for i in reversed(range(len(pa))):
        for j in reversed(range(len(pb))):
            if i + j < order:
                t = lax.dot_general(pa[i], pb[j], dims, preferred_element_type=F32)
                acc = t if acc is None else acc + t
    return acc


def _dot(a, b, dims=_NN):
    return lax.dot_general(a, b, dims, preferred_element_type=F32)


def _iota(shape, dim):
    return lax.broadcasted_iota(jnp.int32, shape, dim)


def _top3(gate, nvalid, axis):
    blk = _iota(gate.shape, axis)
    g = jnp.where(blk < nvalid, gate, NEG_INF)
    sels = []
    for _ in range(MOBA_TOPK):
        m = jnp.max(g, axis=axis, keepdims=True)
        idx = jnp.min(jnp.where(g == m, blk, jnp.int32(1 << 30)), axis=axis, keepdims=True)
        sels.append(jnp.where(m > 0.5 * NEG_INF, idx, -1))
        g = jnp.where(blk == idx, NEG_INF, g)
    return sels


def _norm_proj_kernel(x_ref, g_ref, w_ref, cos_ref, sna_ref, snb_ref, q_ref, k_ref, v_ref, rw_ref, *, att):
    x = x_ref[...]
    ms = jnp.mean(x * x, axis=-1, keepdims=True)
    h = (x * lax.rsqrt(ms + RMS_EPS) * g_ref[...]).astype(BF16)
    cos, sna, snb = cos_ref[...], sna_ref[...], snb_ref[...]
    half = ROT_DIM // 2

    def rope(p):
        return p * cos + pltpu.roll(p, att - half, 1) * sna + pltpu.roll(p, half, 1) * snb

    q_ref[...] = rope(_dot(h, w_ref[:, 0:att]))
    k_ref[...] = rope(_dot(h, w_ref[:, att:2 * att]))
    v_ref[...] = _dot(h, w_ref[:, 2 * att:3 * att])
    rw_ref[...] = _dot(h, w_ref[:, 3 * att:])


def _norm_proj(x2d, g, w_b, tabs, tm, att):
    n, d = x2d.shape
    cols = w_b.shape[1]
    rwc = cols - 3 * att
    ntab = tabs[0].shape[0] // tm
    row = lambda i: (i, 0)
    tab = lambda i: (i % ntab, 0)
    const = lambda i: (0, 0)
    return pl.pallas_call(
        functools.partial(_norm_proj_kernel, att=att),
        grid=(n // tm,),
        in_specs=[
            pl.BlockSpec((tm, d), row),
            pl.BlockSpec((1, d), const),
            pl.BlockSpec((d, cols), const),
            pl.BlockSpec((tm, att), tab),
            pl.BlockSpec((tm, att), tab),
            pl.BlockSpec((tm, att), tab),
        ],
        out_specs=[
            pl.BlockSpec((tm, att), row),
            pl.BlockSpec((tm, att), row),
            pl.BlockSpec((tm, att), row),
            pl.BlockSpec((tm, rwc), row),
        ],
        out_shape=[
            jax.ShapeDtypeStruct((n, att), F32),
            jax.ShapeDtypeStruct((n, att), F32),
            jax.ShapeDtypeStruct((n, att), F32),
            jax.ShapeDtypeStruct((n, rwc), F32),
        ],
        compiler_params=pltpu.CompilerParams(
            dimension_semantics=("arbitrary",), vmem_limit_bytes=VMEM_LIMIT),
        name="norm_proj",
    )(x2d, g, w_b, *tabs)


def _rope_tables(pos, att):
    half = ROT_DIM // 2
    inv = jnp.float32(ROPE_THETA) ** (-jnp.arange(half, dtype=F32) / half)
    ang = pos.astype(F32)[:, None] * inv[None, :]
    cos, sin = jnp.cos(ang), jnp.sin(ang)
    t = pos.shape[0]
    rest = HEAD_DIM - ROT_DIM
    cos_h = jnp.concatenate([cos, cos, jnp.ones((t, rest), F32)], axis=1)
    sna_h = jnp.concatenate([-sin, jnp.zeros((t, half + rest), F32)], axis=1)
    snb_h = jnp.concatenate([jnp.zeros((t, half), F32), sin, jnp.zeros((t, rest), F32)], axis=1)
    nh = att // HEAD_DIM
    return tuple(jnp.tile(a, (1, nh)) for a in (cos_h, sna_h, snb_h))


def _moba_prompt_kernel(q_ref, k_ref, v_ref, o_ref, kb_ref, vt_ref, kmean_ref, *, nb):
    i = pl.program_id(2)
    blk = MOBA_BLOCK

    @pl.when(i == 0)
    def _():
        for j in range(nb):
            kj = k_ref[0, j * blk:(j + 1) * blk, :]
            kmean_ref[j:j + 1, :] = jnp.mean(kj, axis=0, keepdims=True)
            kb_ref[j * blk:(j + 1) * blk, :] = kj.astype(BF16)
            vt_ref[j] = jnp.transpose(v_ref[0, j * blk:(j + 1) * blk, :]).astype(BF16)

    qt = jnp.transpose(q_ref[0]) * (HEAD_DIM ** -0.5)
    row_head = _iota((QUAD, 1), 0) // HEAD_DIM
    causal = _iota((blk, blk), 0) <= _iota((blk, blk), 1)
    kmean = _split(kmean_ref[...], 2)
    k_own = kb_ref[pl.ds(pl.multiple_of(i * blk, blk), blk), :]

    head_rows = [slice(h * HEAD_DIM, (h + 1) * HEAD_DIM) for h in range(QUAD_HEADS)]
    qms = [jnp.where(row_head == h, qt, 0.0) for h in range(QUAD_HEADS)]
    qmbs = [qm.astype(BF16) for qm in qms]
    gates = [_mdot(kmean, _split(qm, 2)) for qm in qms]
    own_scores = [_dot(k_own, qmb) for qmb in qmbs]
    sels = [_top3(g, i, axis=0) for g in gates]
    state = []
    for h in range(QUAD_HEADS):
        s = jnp.where(causal, own_scores[h], NEG_INF)
        m = jnp.max(s, axis=0, keepdims=True)
        p = jnp.exp(s - m)
        l = jnp.sum(p, axis=0, keepdims=True)
        state += [m, l, _dot(vt_ref[i, head_rows[h], :], p.astype(BF16))]

    def body(j, carry):
        kj = kb_ref[pl.ds(pl.multiple_of(j * blk, blk), blk), :]
        scores = [_dot(kj, qmbs[h]) for h in range(QUAD_HEADS)]
        new = []
        for h in range(QUAD_HEADS):
            m, l, acc = carry[3 * h:3 * h + 3]
            s1, s2, s3 = sels[h]
            picked = (s1 == j) | (s2 == j) | (s3 == j)
            s = jnp.where(picked, scores[h], NEG_INF)
            m_new = jnp.maximum(m, jnp.max(s, axis=0, keepdims=True))
            alpha = jnp.exp(m - m_new)
            p = jnp.exp(s - m_new)
            l = alpha * l + jnp.sum(p, axis=0, keepdims=True)
            acc = acc * alpha + _dot(vt_ref[j, head_rows[h], :], p.astype(BF16))
            new += [m_new, l, acc]
        return tuple(new)

    state = lax.fori_loop(0, i, body, tuple(state))
    out_t = jnp.concatenate([state[3 * h + 2] / state[3 * h + 1] for h in range(QUAD_HEADS)], axis=0)
    o_ref[0] = jnp.transpose(out_t)


def _moba_prompt(q, k, v):
    b, t, att = q.shape
    nb = t // MOBA_BLOCK
    nq = att // QUAD
    blk = MOBA_BLOCK
    tile = pl.BlockSpec((1, blk, QUAD), lambda bi, qi, i: (bi, i, qi))
    seq = pl.BlockSpec((1, t, QUAD), lambda bi, qi, i: (bi, 0, qi))
    return pl.pallas_call(
        functools.partial(_moba_prompt_kernel, nb=nb),
        grid=(b, nq, nb),
        in_specs=[tile, seq, seq],
        out_specs=tile,
        out_shape=jax.ShapeDtypeStruct((b, t, att), F32),
        scratch_shapes=[
            pltpu.VMEM((t, QUAD), BF16),
            pltpu.VMEM((nb, QUAD, blk), BF16),
            pltpu.VMEM((nb, QUAD), F32),
        ],
        compiler_params=pltpu.CompilerParams(
            dimension_semantics=("arbitrary", "arbitrary", "arbitrary"), vmem_limit_bytes=VMEM_LIMIT),
        name="moba_prompt",
    )(q, k, v)


PAGES_PER_STEP = 16


def _moba_sample_kernel(pt_ref, qr_ref, qh_ref, kn_ref, vn_ref, *refs, ts, nh, npages, page):
    g_pages = PAGES_PER_STEP
    k_refs = refs[:g_pages]
    v_refs = refs[g_pages:2 * g_pages]
    o_ref = refs[2 * g_pages]
    qbd_ref, kmt_ref, s_ref, sel_ref, m_ref, l_ref, own_ref, acc_ref = refs[2 * g_pages + 1:]
    ph = pl.program_id(1)
    p = pl.program_id(2)
    nsteps = npages // g_pages
    rows = nh * ts
    att = nh * HEAD_DIM
    ppb = MOBA_BLOCK // page
    nblk = npages // ppb
    scale = HEAD_DIM ** -0.5

    @pl.when((ph == 0) & (p == 0))
    def _():
        own_head = (_iota((rows, att), 1) // HEAD_DIM) == (_iota((rows, att), 0) // ts)
        qbd_ref[...] = jnp.where(own_head, qr_ref[0] * scale, 0.0)
        kmt_ref[...] = jnp.zeros_like(kmt_ref)
        acc_ref[...] = jnp.zeros_like(acc_ref)

    @pl.when(ph == 0)
    def _():
        qb = qbd_ref[...].astype(BF16)
        lane = _iota((att, page), 1)
        kmt = kmt_ref[...]
        for gb in range(g_pages // ppb):
            ksum = None
            for u in range(ppb):
                g = gb * ppb + u
                kt = k_refs[g][0, 0].reshape(att, page)
                s_ref[p * g_pages + g] = _dot(qb, kt.astype(BF16))
                ksum = kt if ksum is None else ksum + kt
            blk_id = p * (g_pages // ppb) + gb
            kmt = kmt + jnp.where(lane == blk_id, jnp.sum(ksum, axis=1, keepdims=True), 0.0)
        kmt_ref[...] = kmt

    @pl.when((ph == 0) & (p == nsteps - 1))
    def _():
        gate = _mdot(_split(qbd_ref[...], 2), _split(kmt_ref[...], 2))
        sels = _top3(gate, nblk, axis=1)
        for r in range(MOBA_TOPK):
            sel_ref[r] = sels[r]

        def picked(pg):
            b_ = pg // ppb
            return (sels[0] == b_) | (sels[1] == b_) | (sels[2] == b_)

        def mx(pg, m):
            return jnp.maximum(m, jnp.max(jnp.where(picked(pg), s_ref[pg], NEG_INF), axis=1, keepdims=True))

        m = lax.fori_loop(0, npages, mx, jnp.full((rows, 1), NEG_INF, F32))
        qh = qh_ref[0] * scale
        row_t = _iota((rows, 1), 0) % ts
        s_own = []
        for t in range(ts):
            st = jnp.sum(qh * kn_ref[0, t], axis=1, keepdims=True)
            st = jnp.where(row_t >= t, st, NEG_INF)
            s_own.append(st)
            m = jnp.maximum(m, st)

        def weights(pg, l):
            pe = jnp.where(picked(pg), jnp.exp(s_ref[pg] - m), 0.0)
            s_ref[pg] = pe
            return l + jnp.sum(pe, axis=1, keepdims=True)

        l = lax.fori_loop(0, npages, weights, jnp.zeros((rows, 1), F32))
        own = jnp.zeros((rows, HEAD_DIM), F32)
        for t in range(ts):
            pt_ = jnp.exp(s_own[t] - m)
            l = l + pt_
            own = own + pt_ * vn_ref[0, t]
        l_ref[...] = l
        own_ref[...] = own

    @pl.when(ph == 1)
    def _():
        acc = acc_ref[...]
        for g in range(g_pages):
            vt = v_refs[g][0, 0].reshape(att, page).astype(BF16)
            acc = acc + _dot(s_ref[p * g_pages + g].astype(BF16), vt, _NT)
        acc_ref[...] = acc

    @pl.when((ph == 1) & (p == nsteps - 1))
    def _():
        own_head = (_iota((rows, att), 1) // HEAD_DIM) == (_iota((rows, att), 0) // ts)
        fold = (_iota((att, HEAD_DIM), 0) % HEAD_DIM == _iota((att, HEAD_DIM), 1)).astype(BF16)
        past = _mdot(_split(jnp.where(own_head, acc_ref[...], 0.0), 3), [fold])
        o_ref[0] = (past + own_ref[...]) / l_ref[...]


def _moba_sample(q, k_new, v_new, cache_k, cache_v, page_table, layer):
    db, ts, att = q.shape
    npages = page_table.shape[1]
    page = cache_k.shape[2]
    g_pages = PAGES_PER_STEP
    nsteps = npages // g_pages
    nh = att // HEAD_DIM
    rows = ts * nh
    ppb = MOBA_BLOCK // page
    assert npages % g_pages == 0 and MOBA_BLOCK % page == 0 and g_pages % ppb == 0
    assert npages // ppb <= page

    to_rows = lambda a: a.reshape(db, ts, nh, HEAD_DIM).transpose(0, 2, 1, 3).reshape(db, rows, HEAD_DIM)
    qh = to_rows(q)
    qr = jnp.broadcast_to(q.reshape(db, 1, ts, att), (db, nh, ts, att)).reshape(db, rows, att)
    rep = lambda a: jnp.broadcast_to(a.reshape(db, ts, nh, 1, HEAD_DIM),
                                     (db, ts, nh, ts, HEAD_DIM)).reshape(db, ts, rows, HEAD_DIM)
    kn = rep(k_new)
    vn = rep(v_new)
    ckt = cache_k.transpose(0, 1, 3, 4, 2)
    cvt = cache_v.transpose(0, 1, 3, 4, 2)

    def k_map(g):
        def f(b, ph, p, pt):
            pg = jnp.where(ph == 0, p, nsteps - 1) * g_pages + g
            return (layer, pt[b, pg], 0, 0, 0)
        return f

    def v_map(g):
        def f(b, ph, p, pt):
            pg = jnp.where(ph == 0, 0, p) * g_pages + g
            return (layer, pt[b, pg], 0, 0, 0)
        return f

    tok3 = lambda b, ph, p, pt: (b, 0, 0)
    tok4 = lambda b, ph, p, pt: (b, 0, 0, 0)
    grid_spec = pltpu.PrefetchScalarGridSpec(
        num_scalar_prefetch=1,
        grid=(db, 2, nsteps),
        in_specs=(
            [pl.BlockSpec((1, rows, att), tok3), pl.BlockSpec((1, rows, HEAD_DIM), tok3),
             pl.BlockSpec((1, ts, rows, HEAD_DIM), tok4), pl.BlockSpec((1, ts, rows, HEAD_DIM), tok4)]
            + [pl.BlockSpec((1, 1, nh, HEAD_DIM, page), k_map(g)) for g in range(g_pages)]
            + [pl.BlockSpec((1, 1, nh, HEAD_DIM, page), v_map(g)) for g in range(g_pages)]
        ),
        out_specs=pl.BlockSpec((1, rows, HEAD_DIM), tok3),
        scratch_shapes=[
            pltpu.VMEM((rows, att), F32),
            pltpu.VMEM((att, page), F32),
            pltpu.VMEM((npages, rows, page), F32),
            pltpu.VMEM((MOBA_TOPK, rows, 1), jnp.int32),
            pltpu.VMEM((rows, 1), F32),
            pltpu.VMEM((rows, 1), F32),
            pltpu.VMEM((rows, HEAD_DIM), F32),
            pltpu.VMEM((rows, att), F32),
        ],
    )
    o = pl.pallas_call(
        functools.partial(_moba_sample_kernel, ts=ts, nh=nh, npages=npages, page=page),
        grid_spec=grid_spec,
        out_shape=jax.ShapeDtypeStruct((db, rows, HEAD_DIM), F32),
        compiler_params=pltpu.CompilerParams(
            dimension_semantics=("arbitrary", "arbitrary", "arbitrary"), vmem_limit_bytes=VMEM_LIMIT),
        name="moba_sample",
    )(page_table, qr, qh, kn, vn, *([ckt] * g_pages), *([cvt] * g_pages))
    return o.reshape(db, nh, ts, HEAD_DIM).transpose(0, 2, 1, 3).reshape(db, ts, att)


def _wkv_local(insts, cst):
    c = WKV_CHUNK
    tri, strict, incl, eye_cat, bdmask, hmasks, hbd, eye_q = cst

    def each(f, *lists):
        return [f(*a) for a in zip(*lists)]

    def stack(x):
        xb = x.astype(BF16)
        return jnp.concatenate([jnp.where(hm, xb, jnp.zeros_like(xb)) for hm in hmasks], axis=0)

    def bdiag(x):
        xb = x.astype(BF16)
        return jnp.where(bdmask, jnp.concatenate([xb] * QUAD_HEADS, axis=0), jnp.zeros((QUAD, QUAD), BF16))

    rs, lws, kks, bbs, ks, vs = (list(x) for x in zip(*insts))
    cums = each(lambda lw: _mdot([tri], _split(lw, 3)), lws)
    cls = [cum[c - 1:c, :] for cum in cums]
    kkts = each(lambda kk, cum, lw: kk * jnp.exp(cum - lw), kks, cums, lws)
    rts = each(lambda r, cum: r * jnp.exp(cum), rs, cums)
    e_invs = [jnp.exp(-cum) for cum in cums]
    khs = each(lambda k, e: k * e, ks, e_invs)
    bhs = each(lambda b, e: b * e, bbs, e_invs)
    e_rems = each(lambda cl, cum: jnp.exp(cl - cum), cls, cums)
    kgs = each(lambda k, e: k * e, ks, e_rems)
    bgs = each(lambda b, e: (b * e).astype(BF16), bbs, e_rems)

    lhss = each(lambda kkt, rt: jnp.concatenate([kkt, rt], axis=0).astype(BF16), kkts, rts)
    vsts = [stack(v) for v in vs]
    a_ks = each(lambda lhs, kh: _dot(lhs, stack(kh), _NT), lhss, khs)
    a_bs = each(lambda lhs, bh: _dot(lhs, stack(bh), _NT), lhss, bhs)
    l_ks = [jnp.where(strict, a[:c], 0.0).astype(BF16) for a in a_ks]
    n_bs = [jnp.where(strict, -a[:c], 0.0) for a in a_bs]
    a_rks = [jnp.where(incl, a[c:], 0.0).astype(BF16) for a in a_ks]
    a_rbs = [jnp.where(incl, a[c:], 0.0).astype(BF16) for a in a_bs]

    t_invs = [eye_cat + n for n in n_bs]
    pws = n_bs
    for _ in range(c.bit_length() - 2):
        pws = each(lambda pw: _dot(pw.astype(BF16), bdiag(pw)), pws)
        t_invs = each(lambda t, pw: t + _dot(t.astype(BF16), bdiag(pw)), t_invs, pws)
    t_bs = [t.astype(BF16) for t in t_invs]

    lkvs = each(_dot, l_ks, vsts)
    uks = each(lambda t, kkt: _dot(t, stack(kkt)), t_bs, kkts)
    ucs = each(lambda t, lkv: _dot(t, stack(lkv)), t_bs, lkvs)
    rys = each(lambda rt, a_rb, uk: rt - _dot(a_rb, stack(uk)), rts, a_rbs, uks)
    ycs = each(lambda a_rk, vst, a_rb, uc: _dot(a_rk, vst) - _dot(a_rb, stack(uc)), a_rks, vsts, a_rbs, ucs)
    gcols = [jnp.transpose(jnp.broadcast_to(jnp.exp(cl), (SUBLANES, QUAD)))[:, 0:1] for cl in cls]
    m_mats = each(lambda gcol, bg, uk: jnp.where(eye_q, gcol, 0.0)
                  - jnp.where(hbd, _dot(bg, uk.astype(BF16), _TN), 0.0), gcols, bgs, uks)
    n_mats = each(lambda kg, bg, v, uc: jnp.where(
        hbd, _dot(jnp.concatenate([kg.astype(BF16), -bg], axis=0),
                  jnp.concatenate([v, uc], axis=0).astype(BF16), _TN), 0.0), kgs, bgs, vs, ucs)
    return list(zip(m_mats, n_mats, rys, ycs))


def _rwkv_kernel(rw_ref, sh_ref, h0_ref, mu_ref, w0_ref, wdec_ref, a0_ref, wa_ref, wg_ref, kk_ref, ka_ref,
                 rk_ref, lg_ref, lb_ref, ones_ref, avg_ref, out_ref, hout_ref,
                 h_ref, carry_ref, *, tt, t_valid, rwd):
    t = pl.program_id(1)
    c = WKV_CHUNK
    nq = rwd // QUAD
    nch = tt // c

    @pl.when(t == 0)
    def _():
        h_ref[...] = h0_ref[0]
        carry_ref[...] = sh_ref[0]

    rwf = rw_ref[0]
    row = _iota((tt, 1), 0)
    prev = jnp.where(row == 0, carry_ref[...], pltpu.roll(rwf, 1, 0))
    carry_ref[...] = rwf[tt - 1:tt, :]
    xs = rwf + mu_ref[...] * (prev - rwf)
    r = xs[:, 0:rwd]
    kr = xs[:, rwd:2 * rwd]
    vr = xs[:, 2 * rwd:3 * rwd]
    la = xs[:, 3 * rwd:3 * rwd + LORA_DECAY + LORA_A]
    gd = xs[:, 3 * rwd + LORA_DECAY + LORA_A:]
    w = w0_ref[...] + _mdot(_split(jnp.tanh(la), 2), _split(wdec_ref[...], 2))
    lw = -jnp.exp(-jax.nn.softplus(-w) - 0.5)
    a = jax.nn.sigmoid(a0_ref[...] + _mdot(_split(la, 2), _split(wa_ref[...], 2)))
    g = _mdot(_split(jax.nn.sigmoid(gd), 2), _split(wg_ref[...], 2))
    kk = kr * kk_ref[...]
    km = kr * (1.0 + (a - 1.0) * ka_ref[...])
    ones = ones_ref[...]
    ss = _mdot(_split(kk * kk, 3), [ones])
    kkn = kk / jnp.maximum(jnp.sqrt(ss), 1e-12)
    bb = kkn * a
    if t_valid is not None:
        valid = (t * tt + row) < t_valid
        lw = jnp.where(valid, lw, 0.0)
        kkn = jnp.where(valid, kkn, 0.0)
        bb = jnp.where(valid, bb, 0.0)
        km = jnp.where(valid, km, 0.0)
        vr = jnp.where(valid, vr, 0.0)

    lane_c = _iota((c, QUAD_HEADS * c), 1) % c
    row_c = _iota((c, QUAD_HEADS * c), 0)
    lane_head = _iota((1, QUAD), 1) // HEAD_DIM
    cst = (
        (_iota((c, c), 1) <= _iota((c, c), 0)).astype(BF16),
        lane_c < row_c,
        lane_c <= row_c,
        (lane_c == row_c).astype(F32),
        (_iota((QUAD_HEADS * c, QUAD_HEADS * c), 0) // c) == (_iota((QUAD_HEADS * c, QUAD_HEADS * c), 1) // c),
        [lane_head == h for h in range(QUAD_HEADS)],
        (_iota((QUAD, QUAD), 0) // HEAD_DIM) == (_iota((QUAD, QUAD), 1) // HEAD_DIM),
        _iota((QUAD, QUAD), 0) == _iota((QUAD, QUAD), 1),
    )
    insts = []
    for ci in range(nch):
        rs = slice(ci * c, (ci + 1) * c)
        for qd in range(nq):
            sl = slice(qd * QUAD, (qd + 1) * QUAD)
            insts.append((r[rs, sl], lw[rs, sl], kkn[rs, sl], bb[rs, sl], km[rs, sl], vr[rs, sl]))
    local = _wkv_local(insts, cst)
    hst = [h_ref[qd] for qd in range(nq)]
    ys = [[] for _ in range(nq)]
    for ci in range(nch):
        hb = [h.astype(BF16) for h in hst]
        for qd in range(nq):
            _, _, ry, yc = local[ci * nq + qd]
            ys[qd].append(_dot(ry.astype(BF16), hb[qd]) + yc)
        hst = [_dot(local[ci * nq + qd][0].astype(BF16), hb[qd]) + local[ci * nq + qd][1] for qd in range(nq)]
    for qd in range(nq):
        h_ref[qd] = hst[qd]
    y = jnp.concatenate([jnp.concatenate(col, axis=0) if nch > 1 else col[0] for col in ys], axis=1)
    hout_ref[0] = h_ref[...]

    avg = avg_ref[...]
    mean = _mdot(_split(y, 3), [avg])
    d = y - mean
    var = _mdot(_split(d * d, 3), [avg])
    yn = d * lax.rsqrt(var + GN_EPS) * lg_ref[...] + lb_ref[...]
    bonus = _mdot(_split(r * km * rk_ref[...], 3), [ones]) * vr
    out_ref[0] = (yn + bonus) * g


def _rwkv_mix(rw, shift0, h0, prm, tt, t_valid=None):
    b, t, cols = rw.shape
    rwd = prm["k_k"].shape[1]
    nq = rwd // QUAD
    vec = lambda n: pl.BlockSpec((1, n), lambda bi, ti: (0, 0))
    mat = lambda m, n: pl.BlockSpec((m, n), lambda bi, ti: (0, 0))
    lora_in = LORA_DECAY + LORA_A
    return pl.pallas_call(
        functools.partial(_rwkv_kernel, tt=tt, t_valid=t_valid, rwd=rwd),
        grid=(b, t // tt),
        in_specs=[
            pl.BlockSpec((1, tt, cols), lambda bi, ti: (bi, ti, 0)),
            pl.BlockSpec((1, 1, cols), lambda bi, ti: (bi, 0, 0)),
            pl.BlockSpec((1, nq, QUAD, QUAD), lambda bi, ti: (bi, 0, 0, 0)),
            vec(cols), vec(rwd), mat(lora_in, rwd), vec(rwd), mat(lora_in, rwd), mat(LORA_GATE, rwd),
            vec(rwd), vec(rwd), vec(rwd), vec(rwd), vec(rwd), mat(rwd, rwd), mat(rwd, rwd),
        ],
        out_specs=[
            pl.BlockSpec((1, tt, rwd), lambda bi, ti: (bi, ti, 0)),
            pl.BlockSpec((1, nq, QUAD, QUAD), lambda bi, ti: (bi, 0, 0, 0)),
        ],
        out_shape=[
            jax.ShapeDtypeStruct((b, t, rwd), F32),
            jax.ShapeDtypeStruct((b, nq, QUAD, QUAD), F32),
        ],
        scratch_shapes=[pltpu.VMEM((nq, QUAD, QUAD), F32), pltpu.VMEM((1, cols), F32)],
        compiler_params=pltpu.CompilerParams(
            dimension_semantics=("arbitrary", "arbitrary"), vmem_limit_bytes=VMEM_LIMIT),
        name="rwkv_mix",
    )(rw, shift0, h0, prm["mu"], prm["w0"], prm["wdec"], prm["a0"], prm["wa"], prm["wg"], prm["k_k"],
      prm["k_a"], prm["r_k"], prm["lnx_g"], prm["lnx_b"], prm["ones"], prm["avg"])


def _state_to_blockdiag(s):
    b, h, dv, dk = s.shape
    st = jnp.swapaxes(s, -1, -2).reshape(b, h // QUAD_HEADS, QUAD_HEADS, dk, dv)
    eye = jnp.eye(QUAD_HEADS, dtype=s.dtype)
    return jnp.einsum("bqhkv,hg->bqhkgv", st, eye).reshape(b, h // QUAD_HEADS, QUAD, QUAD)


def _blockdiag_to_state(hb):
    b, nq = hb.shape[:2]
    hr = hb.reshape(b, nq, QUAD_HEADS, HEAD_DIM, QUAD_HEADS, HEAD_DIM)
    st = jnp.einsum("bqhkhv->bqhkv", hr).reshape(b, nq * QUAD_HEADS, HEAD_DIM, HEAD_DIM)
    return jnp.swapaxes(st, -1, -2)


def _out_ffn_kernel(x_ref, a_ref, r_ref, wo_ref, g_ref, wg_ref, wu_ref, wd_ref, o_ref, x1_ref, h2_ref, acc_ref,
                    *, att):
    j = pl.program_id(1)

    @pl.when(j == 0)
    def _():
        x1 = (x_ref[...] + _dot(a_ref[...].astype(BF16), wo_ref[0:att, :])
              + _dot(r_ref[...].astype(BF16), wo_ref[att:, :]))
        x1_ref[...] = x1
        ms = jnp.mean(x1 * x1, axis=-1, keepdims=True)
        h2_ref[...] = (x1 * lax.rsqrt(ms + RMS_EPS) * g_ref[...]).astype(BF16)
        acc_ref[...] = jnp.zeros_like(acc_ref)

    h2 = h2_ref[...]
    gate = _dot(h2, wg_ref[...])
    up = _dot(h2, wu_ref[...])
    act = gate * jax.nn.sigmoid(gate) * up
    acc_ref[...] += _dot(act.astype(BF16), wd_ref[...])

    @pl.when(j == pl.num_programs(1) - 1)
    def _():
        o_ref[...] = x1_ref[...] + acc_ref[...]


def _out_ffn(x2d, attn, rwkv, wo_b, g, wfi_b, wfo_b, tm, hc):
    n, d = x2d.shape
    att = attn.shape[1]
    hid = wfo_b.shape[0]
    nh = hid // hc
    row = lambda i, j: (i, 0)
    const = lambda i, j: (0, 0)
    return pl.pallas_call(
        functools.partial(_out_ffn_kernel, att=att),
        grid=(n // tm, nh),
        in_specs=[
            pl.BlockSpec((tm, d), row),
            pl.BlockSpec((tm, att), row),
            pl.BlockSpec((tm, rwkv.shape[1]), row),
            pl.BlockSpec(wo_b.shape, const),
            pl.BlockSpec((1, d), const),
            pl.BlockSpec((d, hc), lambda i, j: (0, j)),
            pl.BlockSpec((d, hc), lambda i, j: (0, j + nh)),
            pl.BlockSpec((hc, d), lambda i, j: (j, 0)),
        ],
        out_specs=pl.BlockSpec((tm, d), row),
        out_shape=jax.ShapeDtypeStruct((n, d), F32),
        scratch_shapes=[pltpu.VMEM((tm, d), F32), pltpu.VMEM((tm, d), BF16), pltpu.VMEM((tm, d), F32)],
        compiler_params=pltpu.CompilerParams(
            dimension_semantics=("arbitrary", "arbitrary"), vmem_limit_bytes=VMEM_LIMIT),
        name="out_ffn",
    )(x2d, attn, rwkv, wo_b, g, wfi_b, wfi_b, wfo_b)


def _rms_kernel(x_ref, g_ref, o_ref):
    x = x_ref[...]
    ms = jnp.mean(x * x, axis=-1, keepdims=True)
    o_ref[...] = x * lax.rsqrt(ms + RMS_EPS) * g_ref[...]


def _rms_norm(x2d, g, tm):
    n, d = x2d.shape
    return pl.pallas_call(
        _rms_kernel,
        grid=(n // tm,),
        in_specs=[pl.BlockSpec((tm, d), lambda i: (i, 0)), pl.BlockSpec((1, d), lambda i: (0, 0))],
        out_specs=pl.BlockSpec((tm, d), lambda i: (i, 0)),
        out_shape=jax.ShapeDtypeStruct((n, d), F32),
        name="rms_norm",
    )(x2d, g)


def _ffn_chunk(hid):
    best = 128
    for hc in range(128, hid // 2 + 1, 128):
        if hid % hc == 0:
            best = hc
    return best


def kernel(x_prompt, x_sample, cache_k, cache_v, state_wkv, state_shift, page_table, norm_attn_g, w_in, shift_mu, decay_w0, decay_up, iclr_a0, iclr_up, gate_up, k_k, k_a, r_k, lnx_g, lnx_b, w_out, norm_ffn_g, w_ffn_in, w_ffn_out, norm_final_g):
    bsz, seq, d = x_prompt.shape
    db, ts, _ = x_sample.shape
    depth = w_in.shape[0]
    rwd = k_k.shape[1]
    att = (w_in.shape[2] - 3 * rwd - LORA_DECAY - LORA_A - LORA_GATE) // 3
    nheads_att = att // HEAD_DIM
    nheads_rw = rwd // HEAD_DIM
    cols = 3 * rwd + LORA_DECAY + LORA_A + LORA_GATE
    page = cache_k.shape[2]
    npages = page_table.shape[1]
    past_len = npages * page
    hid = w_ffn_out.shape[1]
    hc = _ffn_chunk(hid)

    np_tok = bsz * seq
    ns_tok = db * ts
    tm_p = DENSE_TILE
    tabs_p = _rope_tables(jnp.arange(seq, dtype=jnp.int32), att)
    pos_s = past_len + jnp.arange(ts, dtype=jnp.int32)
    tabs_s = tuple(jnp.tile(a, (db, 1)) for a in _rope_tables(pos_s, att))

    head_of = jnp.arange(rwd) // HEAD_DIM
    same_head = head_of[:, None] == head_of[None, :]
    ones_bd = same_head.astype(BF16)
    avg_bd = (same_head.astype(F32) / HEAD_DIM).astype(BF16)

    ts_pad = WKV_CHUNK
    zeros_shift = jnp.zeros((bsz, 1, cols), F32)
    zeros_state = jnp.zeros((bsz, nheads_rw // QUAD_HEADS, QUAD, QUAD), F32)

    xp = x_prompt.reshape(np_tok, d)
    xs = x_sample.reshape(ns_tok, d)
    outs = {n: [] for n in ("kp", "vp", "sp", "hp", "ks", "vs", "ss", "hs")}
    for l in range(depth):
        w_in_b = w_in[l].astype(BF16)
        wo_b = w_out[l].astype(BF16)
        wfi_b = w_ffn_in[l].astype(BF16)
        wfo_b = w_ffn_out[l].astype(BF16)
        g_attn = norm_attn_g[l].reshape(1, d)
        g_ffn = norm_ffn_g[l].reshape(1, d)
        prm = {
            "mu": shift_mu[l].reshape(1, cols),
            "w0": decay_w0[l].reshape(1, rwd),
            "wdec": jnp.concatenate([decay_up[l], jnp.zeros((LORA_A, rwd), F32)], axis=0),
            "a0": iclr_a0[l].reshape(1, rwd),
            "wa": jnp.concatenate([jnp.zeros((LORA_DECAY, rwd), F32), iclr_up[l]], axis=0),
            "wg": gate_up[l],
            "k_k": k_k[l].reshape(1, rwd),
            "k_a": k_a[l].reshape(1, rwd),
            "r_k": r_k[l].reshape(1, rwd),
            "lnx_g": lnx_g[l].reshape(1, rwd),
            "lnx_b": lnx_b[l].reshape(1, rwd),
            "ones": ones_bd,
            "avg": avg_bd,
        }

        q, k, v, rw = _norm_proj(xp, g_attn, w_in_b, tabs_p, tm_p, att)
        attn = _moba_prompt(q.reshape(bsz, seq, att), k.reshape(bsz, seq, att), v.reshape(bsz, seq, att))
        rw3 = rw.reshape(bsz, seq, cols)
        mix, hfin = _rwkv_mix(rw3, zeros_shift, zeros_state, prm, WKV_TILE)
        xp = _out_ffn(xp, attn.reshape(np_tok, att), mix.reshape(np_tok, rwd), wo_b, g_ffn, wfi_b, wfo_b, tm_p, hc)
        outs["kp"].append(k.reshape(bsz, seq, nheads_att, HEAD_DIM))
        outs["vp"].append(v.reshape(bsz, seq, nheads_att, HEAD_DIM))
        outs["sp"].append(_blockdiag_to_state(hfin))
        outs["hp"].append(rw3[:, -1])

        q, k, v, rw = _norm_proj(xs, g_attn, w_in_b, tabs_s, ns_tok, att)
        q3, k3, v3 = (a.reshape(db, ts, att) for a in (q, k, v))
        attn = _moba_sample(q3, k3, v3, cache_k, cache_v, page_table, l)
        rw3 = rw.reshape(db, ts, cols)
        rw_pad = jnp.pad(rw3, ((0, 0), (0, ts_pad - ts), (0, 0)))
        mix, hfin = _rwkv_mix(rw_pad, state_shift[l].reshape(db, 1, cols), _state_to_blockdiag(state_wkv[l]),
                              prm, ts_pad, t_valid=ts)
        mix = mix[:, :ts].reshape(ns_tok, rwd)
        xs = _out_ffn(xs, attn.reshape(ns_tok, att), mix, wo_b, g_ffn, wfi_b, wfo_b, ns_tok, hc)
        outs["ks"].append(k3.reshape(db, ts, nheads_att, HEAD_DIM))
        outs["vs"].append(v3.reshape(db, ts, nheads_att, HEAD_DIM))
        outs["ss"].append(_blockdiag_to_state(hfin))
        outs["hs"].append(rw3[:, -1])

    g_fin = norm_final_g.reshape(1, d)
    y_prompt = _rms_norm(xp, g_fin, tm_p).reshape(bsz, seq, d)
    y_sample = _rms_norm(xs, g_fin, ns_tok).reshape(db, ts, d)
    st = {n: jnp.stack(v) for n, v in outs.items()}
    return (y_prompt, y_sample, st["kp"], st["vp"], st["sp"], st["hp"], st["ks"], st["vs"], st["ss"], st["hs"])
```

```python
import functools

import jax
import jax.numpy as jnp
from jax import lax
from jax.experimental import pallas as pl
from jax.experimental.pallas import tpu as pltpu

F32 = jnp.float32
BF16 = jnp.bfloat16

HEAD_DIM = 64
MOBA_BLOCK = 256
MOBA_TOPK = 3
ROT_DIM = HEAD_DIM // 4
ROPE_THETA = 500000.0
LORA_DECAY = 64
LORA_A = 64
LORA_GATE = 128
RMS_EPS = 1e-6
GN_EPS = 64e-5
NEG_INF = -1e30

QUAD_HEADS = 4
QUAD = QUAD_HEADS * HEAD_DIM
WKV_CHUNK = 64
WKV_TILE = 256
SUBLANES = 8
DENSE_TILE = 512
VMEM_LIMIT = 56 * 1024 * 1024

_NN = (((1,), (0,)), ((), ()))
_NT = (((1,), (1,)), ((), ()))
_TN = (((0,), (0,)), ((), ()))


def _split(x, n):
    parts = []
    for i in range(n):
        p = x.astype(BF16)
        parts.append(p)
        if i + 1 < n:
            x = x - p.astype(F32)
    return parts


def _mdot(pa, pb, dims=_NN):
    order = max(len(pa), len(pb))
    acc = None
    for i in reversed(range(len(pa))):
        for j in reversed(range(len(pb))):
            if i + j < order:
                t = lax.dot_general(pa[i], pb[j], dims, preferred_element_type=F32)
                acc = t if acc is None else acc + t
    return acc


def _dot(a, b, dims=_NN):
    return lax.dot_general(a, b, dims, preferred_element_type=F32)


def _iota(shape, dim):
    return lax.broadcasted_iota(jnp.int32, shape, dim)


def _top3(gate, nvalid, axis):
    blk = _iota(gate.shape, axis)
    g = jnp.where(blk < nvalid, gate, NEG_INF)
    sels = []
    for _ in range(MOBA_TOPK):
        m = jnp.max(g, axis=axis, keepdims=True)
        idx = jnp.min(jnp.where(g == m, blk, jnp.int32(1 << 30)), axis=axis, keepdims=True)
        sels.append(jnp.where(m > 0.5 * NEG_INF, idx, -1))
        g = jnp.where(blk == idx, NEG_INF, g)
    return sels


def _norm_proj_kernel(x_ref, g_ref, w_ref, cos_ref, sna_ref, snb_ref, *rest, att, stacked):
    if stacked:
        q_ref, k_ref, v_ref, rw_ref, kt_ref, vt_ref = rest[2:]
    else:
        q_ref, k_ref, v_ref, rw_ref = rest
    x = x_ref[...]
    ms = jnp.mean(x * x, axis=-1, keepdims=True)
    h = (x * lax.rsqrt(ms + RMS_EPS) * g_ref[...]).astype(BF16)
    cos, sna, snb = cos_ref[...], sna_ref[...], snb_ref[...]
    half = ROT_DIM // 2

    def rope(p):
        return p * cos + pltpu.roll(p, att - half, 1) * sna + pltpu.roll(p, half, 1) * snb

    q_ref[...] = rope(_dot(h, w_ref[:, 0:att]))
    k = rope(_dot(h, w_ref[:, att:2 * att]))
    v = _dot(h, w_ref[:, 2 * att:3 * att])
    k_ref[...] = k
    v_ref[...] = v
    rw_ref[...] = _dot(h, w_ref[:, 3 * att:])
    if stacked:
        kt_ref[0, 0] = jnp.transpose(k)
        vt_ref[0, 0] = jnp.transpose(v)


def _norm_proj(x2d, g, w_b, tabs, tm, att, stacked=None):
    n, d = x2d.shape
    cols = w_b.shape[1]
    rwc = cols - 3 * att
    ntab = tabs[0].shape[0] // tm
    row = lambda i: (i, 0)
    tab = lambda i: (i % ntab, 0)
    const = lambda i: (0, 0)
    in_specs = [
        pl.BlockSpec((tm, d), row),
        pl.BlockSpec((1, d), const),
        pl.BlockSpec((d, cols), const),
        pl.BlockSpec((tm, att), tab),
        pl.BlockSpec((tm, att), tab),
        pl.BlockSpec((tm, att), tab),
    ]
    out_specs = [
        pl.BlockSpec((tm, att), row),
        pl.BlockSpec((tm, att), row),
        pl.BlockSpec((tm, att), row),
        pl.BlockSpec((tm, rwc), row),
    ]
    out_shape = [
        jax.ShapeDtypeStruct((n, att), F32),
        jax.ShapeDtypeStruct((n, att), F32),
        jax.ShapeDtypeStruct((n, att), F32),
        jax.ShapeDtypeStruct((n, rwc), F32),
    ]
    args = [x2d, g, w_b, *tabs]
    aliases = {}
    if stacked is not None:
        kt_buf, vt_buf, layer = stacked
        slab = pl.BlockSpec((1, 1, att, tm), lambda i: (layer, i // ntab, 0, i % ntab))
        in_specs += [pl.BlockSpec(memory_space=pl.ANY)] * 2
        out_specs += [slab, slab]
        out_shape += [jax.ShapeDtypeStruct(kt_buf.shape, F32), jax.ShapeDtypeStruct(vt_buf.shape, F32)]
        aliases = {len(args): 4, len(args) + 1: 5}
        args += [kt_buf, vt_buf]
    return pl.pallas_call(
        functools.partial(_norm_proj_kernel, att=att, stacked=stacked is not None),
        grid=(n // tm,),
        in_specs=in_specs,
        out_specs=out_specs,
        out_shape=out_shape,
        input_output_aliases=aliases,
        compiler_params=pltpu.CompilerParams(
            dimension_semantics=("arbitrary",), vmem_limit_bytes=VMEM_LIMIT),
        name="norm_proj",
    )(*args)


def _rope_tables(pos, att):
    half = ROT_DIM // 2
    inv = jnp.float32(ROPE_THETA) ** (-jnp.arange(half, dtype=F32) / half)
    ang = pos.astype(F32)[:, None] * inv[None, :]
    cos, sin = jnp.cos(ang), jnp.sin(ang)
    t = pos.shape[0]
    rest = HEAD_DIM - ROT_DIM
    cos_h = jnp.concatenate([cos, cos, jnp.ones((t, rest), F32)], axis=1)
    sna_h = jnp.concatenate([-sin, jnp.zeros((t, half + rest), F32)], axis=1)
    snb_h = jnp.concatenate([jnp.zeros((t, half), F32), sin, jnp.zeros((t, rest), F32)], axis=1)
    nh = att // HEAD_DIM
    return tuple(jnp.tile(a, (1, nh)) for a in (cos_h, sna_h, snb_h))


def _moba_prompt_kernel(q_ref, k_ref, v_ref, o_ref, kb_ref, vt_ref, kmean_ref, *, nb):
    i = pl.program_id(2)
    blk = MOBA_BLOCK

    @pl.when(i == 0)
    def _():
        for j in range(nb):
            kj = k_ref[0, j * blk:(j + 1) * blk, :]
            kmean_ref[j:j + 1, :] = jnp.mean(kj, axis=0, keepdims=True)
            kb_ref[j * blk:(j + 1) * blk, :] = kj.astype(BF16)
            vt_ref[j] = jnp.transpose(v_ref[0, j * blk:(j + 1) * blk, :]).astype(BF16)

    qt = jnp.transpose(q_ref[0]) * (HEAD_DIM ** -0.5)
    row_head = _iota((QUAD, 1), 0) // HEAD_DIM
    causal = _iota((blk, blk), 0) <= _iota((blk, blk), 1)
    kmean = _split(kmean_ref[...], 2)
    k_own = kb_ref[pl.ds(pl.multiple_of(i * blk, blk), blk), :]

    head_rows = [slice(h * HEAD_DIM, (h + 1) * HEAD_DIM) for h in range(QUAD_HEADS)]
    qms = [jnp.where(row_head == h, qt, 0.0) for h in range(QUAD_HEADS)]
    qmbs = [qm.astype(BF16) for qm in qms]
    gates = [_mdot(kmean, _split(qm, 2)) for qm in qms]
    own_scores = [_dot(k_own, qmb) for qmb in qmbs]
    sels = [_top3(g, i, axis=0) for g in gates]
    state = []
    for h in range(QUAD_HEADS):
        s = jnp.where(causal, own_scores[h], NEG_INF)
        m = jnp.max(s, axis=0, keepdims=True)
        p = jnp.exp(s - m)
        l = jnp.sum(p, axis=0, keepdims=True)
        state += [m, l, _dot(vt_ref[i, head_rows[h], :], p.astype(BF16))]

    def body(j, carry):
        kj = kb_ref[pl.ds(pl.multiple_of(j * blk, blk), blk), :]
        scores = [_dot(kj, qmbs[h]) for h in range(QUAD_HEADS)]
        new = []
        for h in range(QUAD_HEADS):
            m, l, acc = carry[3 * h:3 * h + 3]
            s1, s2, s3 = sels[h]
            picked = (s1 == j) | (s2 == j) | (s3 == j)
            s = jnp.where(picked, scores[h], NEG_INF)
            m_new = jnp.maximum(m, jnp.max(s, axis=0, keepdims=True))
            alpha = jnp.exp(m - m_new)
            p = jnp.exp(s - m_new)
            l = alpha * l + jnp.sum(p, axis=0, keepdims=True)
            acc = acc * alpha + _dot(vt_ref[j, head_rows[h], :], p.astype(BF16))
            new += [m_new, l, acc]
        return tuple(new)

    state = lax.fori_loop(0, i, body, tuple(state))
    out_t = jnp.concatenate([state[3 * h + 2] / state[3 * h + 1] for h in range(QUAD_HEADS)], axis=0)
    o_ref[0] = jnp.transpose(out_t)


def _moba_prompt(q, k, v):
    b, t, att = q.shape
    nb = t // MOBA_BLOCK
    nq = att // QUAD
    blk = MOBA_BLOCK
    tile = pl.BlockSpec((1, blk, QUAD), lambda bi, qi, i: (bi, i, qi))
    seq = pl.BlockSpec((1, t, QUAD), lambda bi, qi, i: (bi, 0, qi))
    return pl.pallas_call(
        functools.partial(_moba_prompt_kernel, nb=nb),
        grid=(b, nq, nb),
        in_specs=[tile, seq, seq],
        out_specs=tile,
        out_shape=jax.ShapeDtypeStruct((b, t, att), F32),
        scratch_shapes=[
            pltpu.VMEM((t, QUAD), BF16),
            pltpu.VMEM((nb, QUAD, blk), BF16),
            pltpu.VMEM((nb, QUAD), F32),
        ],
        compiler_params=pltpu.CompilerParams(
            dimension_semantics=("arbitrary", "arbitrary", "arbitrary"), vmem_limit_bytes=VMEM_LIMIT),
        name="moba_prompt",
    )(q, k, v)


PAGES_PER_STEP = 16


def _moba_sample_kernel(pt_ref, qr_ref, qh_ref, kn_ref, vn_ref, *refs, ts, nh, npages, page):
    g_pages = PAGES_PER_STEP
    k_refs = refs[:g_pages]
    v_refs = refs[g_pages:2 * g_pages]
    o_ref = refs[2 * g_pages]
    qbd_ref, kmt_ref, s_ref, sel_ref, m_ref, l_ref, own_ref, acc_ref = refs[2 * g_pages + 1:]
    ph = pl.program_id(1)
    p = pl.program_id(2)
    nsteps = npages // g_pages
    rows = nh * ts
    att = nh * HEAD_DIM
    ppb = MOBA_BLOCK // page
    nblk = npages // ppb
    scale = HEAD_DIM ** -0.5

    @pl.when((ph == 0) & (p == 0))
    def _():
        own_head = (_iota((rows, att), 1) // HEAD_DIM) == (_iota((rows, att), 0) // ts)
        qbd_ref[...] = jnp.where(own_head, qr_ref[0] * scale, 0.0)
        kmt_ref[...] = jnp.zeros_like(kmt_ref)
        acc_ref[...] = jnp.zeros_like(acc_ref)

    @pl.when(ph == 0)
    def _():
        qb = qbd_ref[...].astype(BF16)
        lane = _iota((att, page), 1)
        kmt = kmt_ref[...]
        for gb in range(g_pages // ppb):
            ksum = None
            for u in range(ppb):
                g = gb * ppb + u
                kt = k_refs[g][0, 0].reshape(att, page)
                s_ref[p * g_pages + g] = _dot(qb, kt.astype(BF16))
                ksum = kt if ksum is None else ksum + kt
            blk_id = p * (g_pages // ppb) + gb
            kmt = kmt + jnp.where(lane == blk_id, jnp.sum(ksum, axis=1, keepdims=True), 0.0)
        kmt_ref[...] = kmt

    @pl.when((ph == 0) & (p == nsteps - 1))
    def _():
        gate = _mdot(_split(qbd_ref[...], 2), _split(kmt_ref[...], 2))
        sels = _top3(gate, nblk, axis=1)
        for r in range(MOBA_TOPK):
            sel_ref[r] = sels[r]

        def picked(pg):
            b_ = pg // ppb
            return (sels[0] == b_) | (sels[1] == b_) | (sels[2] == b_)

        def mx(pg, m):
            return jnp.maximum(m, jnp.max(jnp.where(picked(pg), s_ref[pg], NEG_INF), axis=1, keepdims=True))

        m = lax.fori_loop(0, npages, mx, jnp.full((rows, 1), NEG_INF, F32))
        qh = qh_ref[0] * scale
        row_t = _iota((rows, 1), 0) % ts
        s_own = []
        for t in range(ts):
            st = jnp.sum(qh * kn_ref[0, t], axis=1, keepdims=True)
            st = jnp.where(row_t >= t, st, NEG_INF)
            s_own.append(st)
            m = jnp.maximum(m, st)

        def weights(pg, l):
            pe = jnp.where(picked(pg), jnp.exp(s_ref[pg] - m), 0.0)
            s_ref[pg] = pe
            return l + jnp.sum(pe, axis=1, keepdims=True)

        l = lax.fori_loop(0, npages, weights, jnp.zeros((rows, 1), F32))
        own = jnp.zeros((rows, HEAD_DIM), F32)
        for t in range(ts):
            pt_ = jnp.exp(s_own[t] - m)
            l = l + pt_
            own = own + pt_ * vn_ref[0, t]
        l_ref[...] = l
        own_ref[...] = own

    @pl.when(ph == 1)
    def _():
        acc = acc_ref[...]
        for g in range(g_pages):
            vt = v_refs[g][0, 0].reshape(att, page).astype(BF16)
            acc = acc + _dot(s_ref[p * g_pages + g].astype(BF16), vt, _NT)
        acc_ref[...] = acc

    @pl.when((ph == 1) & (p == nsteps - 1))
    def _():
        own_head = (_iota((rows, att), 1) // HEAD_DIM) == (_iota((rows, att), 0) // ts)
        fold = (_iota((att, HEAD_DIM), 0) % HEAD_DIM == _iota((att, HEAD_DIM), 1)).astype(BF16)
        past = _mdot(_split(jnp.where(own_head, acc_ref[...], 0.0), 3), [fold])
        o_ref[0] = (past + own_ref[...]) / l_ref[...]


def _moba_sample(q, k_new, v_new, cache_k, cache_v, page_table, layer):
    db, ts, att = q.shape
    npages = page_table.shape[1]
    page = cache_k.shape[2]
    g_pages = PAGES_PER_STEP
    nsteps = npages // g_pages
    nh = att // HEAD_DIM
    rows = ts * nh
    ppb = MOBA_BLOCK // page
    assert npages % g_pages == 0 and MOBA_BLOCK % page == 0 and g_pages % ppb == 0
    assert npages // ppb <= page

    to_rows = lambda a: a.reshape(db, ts, nh, HEAD_DIM).transpose(0, 2, 1, 3).reshape(db, rows, HEAD_DIM)
    qh = to_rows(q)
    qr = jnp.broadcast_to(q.reshape(db, 1, ts, att), (db, nh, ts, att)).reshape(db, rows, att)
    rep = lambda a: jnp.broadcast_to(a.reshape(db, ts, nh, 1, HEAD_DIM),
                                     (db, ts, nh, ts, HEAD_DIM)).reshape(db, ts, rows, HEAD_DIM)
    kn = rep(k_new)
    vn = rep(v_new)
    ckt = cache_k.transpose(0, 1, 3, 4, 2)
    cvt = cache_v.transpose(0, 1, 3, 4, 2)

    def k_map(g):
        def f(b, ph, p, pt):
            pg = jnp.where(ph == 0, p, nsteps - 1) * g_pages + g
            return (layer, pt[b, pg], 0, 0, 0)
        return f

    def v_map(g):
        def f(b, ph, p, pt):
            pg = jnp.where(ph == 0, 0, p) * g_pages + g
            return (layer, pt[b, pg], 0, 0, 0)
        return f

    tok3 = lambda b, ph, p, pt: (b, 0, 0)
    tok4 = lambda b, ph, p, pt: (b, 0, 0, 0)
    grid_spec = pltpu.PrefetchScalarGridSpec(
        num_scalar_prefetch=1,
        grid=(db, 2, nsteps),
        in_specs=(
            [pl.BlockSpec((1, rows, att), tok3), pl.BlockSpec((1, rows, HEAD_DIM), tok3),
             pl.BlockSpec((1, ts, rows, HEAD_DIM), tok4), pl.BlockSpec((1, ts, rows, HEAD_DIM), tok4)]
            + [pl.BlockSpec((1, 1, nh, HEAD_DIM, page), k_map(g)) for g in range(g_pages)]
            + [pl.BlockSpec((1, 1, nh, HEAD_DIM, page), v_map(g)) for g in range(g_pages)]
        ),
        out_specs=pl.BlockSpec((1, rows, HEAD_DIM), tok3),
        scratch_shapes=[
            pltpu.VMEM((rows, att), F32),
            pltpu.VMEM((att, page), F32),
            pltpu.VMEM((npages, rows, page), F32),
            pltpu.VMEM((MOBA_TOPK, rows, 1), jnp.int32),
            pltpu.VMEM((rows, 1), F32),
            pltpu.VMEM((rows, 1), F32),
            pltpu.VMEM((rows, HEAD_DIM), F32),
            pltpu.VMEM((rows, att), F32),
        ],
    )
    o = pl.pallas_call(
        functools.partial(_moba_sample_kernel, ts=ts, nh=nh, npages=npages, page=page),
        grid_spec=grid_spec,
        out_shape=jax.ShapeDtypeStruct((db, rows, HEAD_DIM), F32),
        compiler_params=pltpu.CompilerParams(
            dimension_semantics=("arbitrary", "arbitrary", "arbitrary"), vmem_limit_bytes=VMEM_LIMIT),
        name="moba_sample",
    )(page_table, qr, qh, kn, vn, *([ckt] * g_pages), *([cvt] * g_pages))
    return o.reshape(db, nh, ts, HEAD_DIM).transpose(0, 2, 1, 3).reshape(db, ts, att)


def _wkv_local(insts, cst):
    c = WKV_CHUNK
    tri, strict, incl, eye_cat, bdmask, hmasks, hbd, eye_q = cst

    def each(f, *lists):
        return [f(*a) for a in zip(*lists)]

    def stack(x):
        xb = x.astype(BF16)
        return jnp.concatenate([jnp.where(hm, xb, jnp.zeros_like(xb)) for hm in hmasks], axis=0)

    def bdiag(x):
        xb = x.astype(BF16)
        return jnp.where(bdmask, jnp.concatenate([xb] * QUAD_HEADS, axis=0), jnp.zeros((QUAD, QUAD), BF16))

    rs, lws, kks, bbs, ks, vs = (list(x) for x in zip(*insts))
    cums = each(lambda lw: _mdot([tri], _split(lw, 3)), lws)
    cls = [cum[c - 1:c, :] for cum in cums]
    kkts = each(lambda kk, cum, lw: kk * jnp.exp(cum - lw), kks, cums, lws)
    rts = each(lambda r, cum: r * jnp.exp(cum), rs, cums)
    e_invs = [jnp.exp(-cum) for cum in cums]
    khs = each(lambda k, e: k * e, ks, e_invs)
    bhs = each(lambda b, e: b * e, bbs, e_invs)
    e_rems = each(lambda cl, cum: jnp.exp(cl - cum), cls, cums)
    kgs = each(lambda k, e: k * e, ks, e_rems)
    bgs = each(lambda b, e: (b * e).astype(BF16), bbs, e_rems)

    lhss = each(lambda kkt, rt: jnp.concatenate([kkt, rt], axis=0).astype(BF16), kkts, rts)
    vsts = [stack(v) for v in vs]
    a_ks = each(lambda lhs, kh: _dot(lhs, stack(kh), _NT), lhss, khs)
    a_bs = each(lambda lhs, bh: _dot(lhs, stack(bh), _NT), lhss, bhs)
    l_ks = [jnp.where(strict, a[:c], 0.0).astype(BF16) for a in a_ks]
    n_bs = [jnp.where(strict, -a[:c], 0.0) for a in a_bs]
    a_rks = [jnp.where(incl, a[c:], 0.0).astype(BF16) for a in a_ks]
    a_rbs = [jnp.where(incl, a[c:], 0.0).astype(BF16) for a in a_bs]

    t_invs = [eye_cat + n for n in n_bs]
    pws = n_bs
    for _ in range(c.bit_length() - 2):
        pws = each(lambda pw: _dot(pw.astype(BF16), bdiag(pw)), pws)
        t_invs = each(lambda t, pw: t + _dot(t.astype(BF16), bdiag(pw)), t_invs, pws)
    t_bs = [t.astype(BF16) for t in t_invs]

    lkvs = each(_dot, l_ks, vsts)
    uks = each(lambda t, kkt: _dot(t, stack(kkt)), t_bs, kkts)
    ucs = each(lambda t, lkv: _dot(t, stack(lkv)), t_bs, lkvs)
    rys = each(lambda rt, a_rb, uk: rt - _dot(a_rb, stack(uk)), rts, a_rbs, uks)
    ycs = each(lambda a_rk, vst, a_rb, uc: _dot(a_rk, vst) - _dot(a_rb, stack(uc)), a_rks, vsts, a_rbs, ucs)
    gcols = [jnp.transpose(jnp.broadcast_to(jnp.exp(cl), (SUBLANES, QUAD)))[:, 0:1] for cl in cls]
    m_mats = each(lambda gcol, bg, uk: jnp.where(eye_q, gcol, 0.0)
                  - jnp.where(hbd, _dot(bg, uk.astype(BF16), _TN), 0.0), gcols, bgs, uks)
    n_mats = each(lambda kg, bg, v, uc: jnp.where(
        hbd, _dot(jnp.concatenate([kg.astype(BF16), -bg], axis=0),
                  jnp.concatenate([v, uc], axis=0).astype(BF16), _TN), 0.0), kgs, bgs, vs, ucs)
    return list(zip(m_mats, n_mats, rys, ycs))


def _rwkv_kernel(rw_ref, sh_ref, h0_ref, mu_ref, w0_ref, wdec_ref, a0_ref, wa_ref, wg_ref, kk_ref, ka_ref,
                 rk_ref, lg_ref, lb_ref, ones_ref, avg_ref, out_ref, hout_ref,
                 h_ref, carry_ref, *, tt, t_valid, rwd):
    t = pl.program_id(1)
    c = WKV_CHUNK
    nq = rwd // QUAD
    nch = tt // c

    @pl.when(t == 0)
    def _():
        h_ref[...] = h0_ref[0]
        carry_ref[...] = sh_ref[0]

    rwf = rw_ref[0]
    row = _iota((tt, 1), 0)
    prev = jnp.where(row == 0, carry_ref[...], pltpu.roll(rwf, 1, 0))
    carry_ref[...] = rwf[tt - 1:tt, :]
    xs = rwf + mu_ref[...] * (prev - rwf)
    r = xs[:, 0:rwd]
    kr = xs[:, rwd:2 * rwd]
    vr = xs[:, 2 * rwd:3 * rwd]
    la = xs[:, 3 * rwd:3 * rwd + LORA_DECAY + LORA_A]
    gd = xs[:, 3 * rwd + LORA_DECAY + LORA_A:]
    w = w0_ref[...] + _mdot(_split(jnp.tanh(la), 2), _split(wdec_ref[...], 2))
    lw = -jnp.exp(-jax.nn.softplus(-w) - 0.5)
    a = jax.nn.sigmoid(a0_ref[...] + _mdot(_split(la, 2), _split(wa_ref[...], 2)))
    g = _mdot(_split(jax.nn.sigmoid(gd), 2), _split(wg_ref[...], 2))
    kk = kr * kk_ref[...]
    km = kr * (1.0 + (a - 1.0) * ka_ref[...])
    ones = ones_ref[...]
    ss = _mdot(_split(kk * kk, 3), [ones])
    kkn = kk / jnp.maximum(jnp.sqrt(ss), 1e-12)
    bb = kkn * a
    if t_valid is not None:
        valid = (t * tt + row) < t_valid
        lw = jnp.where(valid, lw, 0.0)
        kkn = jnp.where(valid, kkn, 0.0)
        bb = jnp.where(valid, bb, 0.0)
        km = jnp.where(valid, km, 0.0)
        vr = jnp.where(valid, vr, 0.0)

    lane_c = _iota((c, QUAD_HEADS * c), 1) % c
    row_c = _iota((c, QUAD_HEADS * c), 0)
    lane_head = _iota((1, QUAD), 1) // HEAD_DIM
    cst = (
        (_iota((c, c), 1) <= _iota((c, c), 0)).astype(BF16),
        lane_c < row_c,
        lane_c <= row_c,
        (lane_c == row_c).astype(F32),
        (_iota((QUAD_HEADS * c, QUAD_HEADS * c), 0) // c) == (_iota((QUAD_HEADS * c, QUAD_HEADS * c), 1) // c),
        [lane_head == h for h in range(QUAD_HEADS)],
        (_iota((QUAD, QUAD), 0) // HEAD_DIM) == (_iota((QUAD, QUAD), 1) // HEAD_DIM),
        _iota((QUAD, QUAD), 0) == _iota((QUAD, QUAD), 1),
    )
    insts = []
    for ci in range(nch):
        rs = slice(ci * c, (ci + 1) * c)
        for qd in range(nq):
            sl = slice(qd * QUAD, (qd + 1) * QUAD)
            insts.append((r[rs, sl], lw[rs, sl], kkn[rs, sl], bb[rs, sl], km[rs, sl], vr[rs, sl]))
    local = _wkv_local(insts, cst)
    hst = [h_ref[qd] for qd in range(nq)]
    ys = [[] for _ in range(nq)]
    for ci in range(nch):
        hb = [h.astype(BF16) for h in hst]
        for qd in range(nq):
            _, _, ry, yc = local[ci * nq + qd]
            ys[qd].append(_dot(ry.astype(BF16), hb[qd]) + yc)
        hst = [_dot(local[ci * nq + qd][0].astype(BF16), hb[qd]) + local[ci * nq + qd][1] for qd in range(nq)]
    for qd in range(nq):
        h_ref[qd] = hst[qd]
    y = jnp.concatenate([jnp.concatenate(col, axis=0) if nch > 1 else col[0] for col in ys], axis=1)
    hout_ref[0] = h_ref[...]

    avg = avg_ref[...]
    mean = _mdot(_split(y, 3), [avg])
    d = y - mean
    var = _mdot(_split(d * d, 3), [avg])
    yn = d * lax.rsqrt(var + GN_EPS) * lg_ref[...] + lb_ref[...]
    bonus = _mdot(_split(r * km * rk_ref[...], 3), [ones]) * vr
    out_ref[0] = (yn + bonus) * g


def _rwkv_mix(rw, shift0, h0, prm, tt, t_valid=None):
    b, t, cols = rw.shape
    rwd = prm["k_k"].shape[1]
    nq = rwd // QUAD
    vec = lambda n: pl.BlockSpec((1, n), lambda bi, ti: (0, 0))
    mat = lambda m, n: pl.BlockSpec((m, n), lambda bi, ti: (0, 0))
    lora_in = LORA_DECAY + LORA_A
    return pl.pallas_call(
        functools.partial(_rwkv_kernel, tt=tt, t_valid=t_valid, rwd=rwd),
        grid=(b, t // tt),
        in_specs=[
            pl.BlockSpec((1, tt, cols), lambda bi, ti: (bi, ti, 0)),
            pl.BlockSpec((1, 1, cols), lambda bi, ti: (bi, 0, 0)),
            pl.BlockSpec((1, nq, QUAD, QUAD), lambda bi, ti: (bi, 0, 0, 0)),
            vec(cols), vec(rwd), mat(lora_in, rwd), vec(rwd), mat(lora_in, rwd), mat(LORA_GATE, rwd),
            vec(rwd), vec(rwd), vec(rwd), vec(rwd), vec(rwd), mat(rwd, rwd), mat(rwd, rwd),
        ],
        out_specs=[
            pl.BlockSpec((1, tt, rwd), lambda bi, ti: (bi, ti, 0)),
            pl.BlockSpec((1, nq, QUAD, QUAD), lambda bi, ti: (bi, 0, 0, 0)),
        ],
        out_shape=[
            jax.ShapeDtypeStruct((b, t, rwd), F32),
            jax.ShapeDtypeStruct((b, nq, QUAD, QUAD), F32),
        ],
        scratch_shapes=[pltpu.VMEM((nq, QUAD, QUAD), F32), pltpu.VMEM((1, cols), F32)],
        compiler_params=pltpu.CompilerParams(
            dimension_semantics=("arbitrary", "arbitrary"), vmem_limit_bytes=VMEM_LIMIT),
        name="rwkv_mix",
    )(rw, shift0, h0, prm["mu"], prm["w0"], prm["wdec"], prm["a0"], prm["wa"], prm["wg"], prm["k_k"],
      prm["k_a"], prm["r_k"], prm["lnx_g"], prm["lnx_b"], prm["ones"], prm["avg"])


def _state_to_blockdiag(s):
    b, h, dv, dk = s.shape
    st = jnp.swapaxes(s, -1, -2).reshape(b, h // QUAD_HEADS, QUAD_HEADS, dk, dv)
    eye = jnp.eye(QUAD_HEADS, dtype=s.dtype)
    return jnp.einsum("bqhkv,hg->bqhkgv", st, eye).reshape(b, h // QUAD_HEADS, QUAD, QUAD)


def _blockdiag_to_state(hb):
    b, nq = hb.shape[:2]
    hr = hb.reshape(b, nq, QUAD_HEADS, HEAD_DIM, QUAD_HEADS, HEAD_DIM)
    st = jnp.einsum("bqhkhv->bqhkv", hr).reshape(b, nq * QUAD_HEADS, HEAD_DIM, HEAD_DIM)
    return jnp.swapaxes(st, -1, -2)


def _out_ffn_kernel(x_ref, a_ref, r_ref, wo_ref, g_ref, wg_ref, wu_ref, wd_ref, o_ref, x1_ref, h2_ref, acc_ref,
                    *, att):
    j = pl.program_id(1)

    @pl.when(j == 0)
    def _():
        x1 = (x_ref[...] + _dot(a_ref[...].astype(BF16), wo_ref[0:att, :])
              + _dot(r_ref[...].astype(BF16), wo_ref[att:, :]))
        x1_ref[...] = x1
        ms = jnp.mean(x1 * x1, axis=-1, keepdims=True)
        h2_ref[...] = (x1 * lax.rsqrt(ms + RMS_EPS) * g_ref[...]).astype(BF16)
        acc_ref[...] = jnp.zeros_like(acc_ref)

    h2 = h2_ref[...]
    gate = _dot(h2, wg_ref[...])
    up = _dot(h2, wu_ref[...])
    act = gate * jax.nn.sigmoid(gate) * up
    acc_ref[...] += _dot(act.astype(BF16), wd_ref[...])

    @pl.when(j == pl.num_programs(1) - 1)
    def _():
        o_ref[...] = x1_ref[...] + acc_ref[...]


def _out_ffn(x2d, attn, rwkv, wo_b, g, wfi_b, wfo_b, tm, hc):
    n, d = x2d.shape
    att = attn.shape[1]
    hid = wfo_b.shape[0]
    nh = hid // hc
    row = lambda i, j: (i, 0)
    const = lambda i, j: (0, 0)
    return pl.pallas_call(
        functools.partial(_out_ffn_kernel, att=att),
        grid=(n // tm, nh),
        in_specs=[
            pl.BlockSpec((tm, d), row),
            pl.BlockSpec((tm, att), row),
            pl.BlockSpec((tm, rwkv.shape[1]), row),
            pl.BlockSpec(wo_b.shape, const),
            pl.BlockSpec((1, d), const),
            pl.BlockSpec((d, hc), lambda i, j: (0, j)),
            pl.BlockSpec((d, hc), lambda i, j: (0, j + nh)),
            pl.BlockSpec((hc, d), lambda i, j: (j, 0)),
        ],
        out_specs=pl.BlockSpec((tm, d), row),
        out_shape=jax.ShapeDtypeStruct((n, d), F32),
        scratch_shapes=[pltpu.VMEM((tm, d), F32), pltpu.VMEM((tm, d), BF16), pltpu.VMEM((tm, d), F32)],
        compiler_params=pltpu.CompilerParams(
            dimension_semantics=("arbitrary", "arbitrary"), vmem_limit_bytes=VMEM_LIMIT),
        name="out_ffn",
    )(x2d, attn, rwkv, wo_b, g, wfi_b, wfi_b, wfo_b)


def _rms_kernel(x_ref, g_ref, o_ref):
    x = x_ref[...]
    ms = jnp.mean(x * x, axis=-1, keepdims=True)
    o_ref[...] = x * lax.rsqrt(ms + RMS_EPS) * g_ref[...]


def _rms_norm(x2d, g, tm):
    n, d = x2d.shape
    return pl.pallas_call(
        _rms_kernel,
        grid=(n // tm,),
        in_specs=[pl.BlockSpec((tm, d), lambda i: (i, 0)), pl.BlockSpec((1, d), lambda i: (0, 0))],
        out_specs=pl.BlockSpec((tm, d), lambda i: (i, 0)),
        out_shape=jax.ShapeDtypeStruct((n, d), F32),
        name="rms_norm",
    )(x2d, g)


def _ffn_chunk(hid):
    best = 128
    for hc in range(128, hid // 2 + 1, 128):
        if hid % hc == 0:
            best = hc
    return best


def kernel(x_prompt, x_sample, cache_k, cache_v, state_wkv, state_shift, page_table, norm_attn_g, w_in, shift_mu, decay_w0, decay_up, iclr_a0, iclr_up, gate_up, k_k, k_a, r_k, lnx_g, lnx_b, w_out, norm_ffn_g, w_ffn_in, w_ffn_out, norm_final_g):
    bsz, seq, d = x_prompt.shape
    db, ts, _ = x_sample.shape
    depth = w_in.shape[0]
    rwd = k_k.shape[1]
    att = (w_in.shape[2] - 3 * rwd - LORA_DECAY - LORA_A - LORA_GATE) // 3
    nheads_att = att // HEAD_DIM
    nheads_rw = rwd // HEAD_DIM
    cols = 3 * rwd + LORA_DECAY + LORA_A + LORA_GATE
    page = cache_k.shape[2]
    npages = page_table.shape[1]
    past_len = npages * page
    hid = w_ffn_out.shape[1]
    hc = _ffn_chunk(hid)

    np_tok = bsz * seq
    ns_tok = db * ts
    tm_p = DENSE_TILE
    tabs_p = _rope_tables(jnp.arange(seq, dtype=jnp.int32), att)
    pos_s = past_len + jnp.arange(ts, dtype=jnp.int32)
    tabs_s = tuple(jnp.tile(a, (db, 1)) for a in _rope_tables(pos_s, att))

    head_of = jnp.arange(rwd) // HEAD_DIM
    same_head = head_of[:, None] == head_of[None, :]
    ones_bd = same_head.astype(BF16)
    avg_bd = (same_head.astype(F32) / HEAD_DIM).astype(BF16)

    ts_pad = WKV_CHUNK
    zeros_shift = jnp.zeros((bsz, 1, cols), F32)
    zeros_state = jnp.zeros((bsz, nheads_rw // QUAD_HEADS, QUAD, QUAD), F32)

    xp = x_prompt.reshape(np_tok, d)
    xs = x_sample.reshape(ns_tok, d)
    kt_all = jnp.zeros((depth, bsz, att, seq), F32)
    vt_all = jnp.zeros((depth, bsz, att, seq), F32)
    outs = {n: [] for n in ("sp", "hp", "ks", "vs", "ss", "hs")}
    for l in range(depth):
        w_in_b = w_in[l].astype(BF16)
        wo_b = w_out[l].astype(BF16)
        wfi_b = w_ffn_in[l].astype(BF16)
        wfo_b = w_ffn_out[l].astype(BF16)
        g_attn = norm_attn_g[l].reshape(1, d)
        g_ffn = norm_ffn_g[l].reshape(1, d)
        prm = {
            "mu": shift_mu[l].reshape(1, cols),
            "w0": decay_w0[l].reshape(1, rwd),
            "wdec": jnp.concatenate([decay_up[l], jnp.zeros((LORA_A, rwd), F32)], axis=0),
            "a0": iclr_a0[l].reshape(1, rwd),
            "wa": jnp.concatenate([jnp.zeros((LORA_DECAY, rwd), F32), iclr_up[l]], axis=0),
            "wg": gate_up[l],
            "k_k": k_k[l].reshape(1, rwd),
            "k_a": k_a[l].reshape(1, rwd),
            "r_k": r_k[l].reshape(1, rwd),
            "lnx_g": lnx_g[l].reshape(1, rwd),
            "lnx_b": lnx_b[l].reshape(1, rwd),
            "ones": ones_bd,
            "avg": avg_bd,
        }

        q, k, v, rw, kt_all, vt_all = _norm_proj(xp, g_attn, w_in_b, tabs_p, tm_p, att, (kt_all, vt_all, l))
        attn = _moba_prompt(q.reshape(bsz, seq, att), k.reshape(bsz, seq, att), v.reshape(bsz, seq, att))
        rw3 = rw.reshape(bsz, seq, cols)
        mix, hfin = _rwkv_mix(rw3, zeros_shift, zeros_state, prm, WKV_TILE)
        xp = _out_ffn(xp, attn.reshape(np_tok, att), mix.reshape(np_tok, rwd), wo_b, g_ffn, wfi_b, wfo_b, tm_p, hc)
        outs["sp"].append(_blockdiag_to_state(hfin))
        outs["hp"].append(rw3[:, -1])

        q, k, v, rw = _norm_proj(xs, g_attn, w_in_b, tabs_s, ns_tok, att)
        q3, k3, v3 = (a.reshape(db, ts, att) for a in (q, k, v))
        attn = _moba_sample(q3, k3, v3, cache_k, cache_v, page_table, l)
        rw3 = rw.reshape(db, ts, cols)
        rw_pad = jnp.pad(rw3, ((0, 0), (0, ts_pad - ts), (0, 0)))
        mix, hfin = _rwkv_mix(rw_pad, state_shift[l].reshape(db, 1, cols), _state_to_blockdiag(state_wkv[l]),
                              prm, ts_pad, t_valid=ts)
        mix = mix[:, :ts].reshape(ns_tok, rwd)
        xs = _out_ffn(xs, attn.reshape(ns_tok, att), mix, wo_b, g_ffn, wfi_b, wfo_b, ns_tok, hc)
        outs["ks"].append(k3.reshape(db, ts, nheads_att, HEAD_DIM))
        outs["vs"].append(v3.reshape(db, ts, nheads_att, HEAD_DIM))
        outs["ss"].append(_blockdiag_to_state(hfin))
        outs["hs"].append(rw3[:, -1])

    g_fin = norm_final_g.reshape(1, d)
    y_prompt = _rms_norm(xp, g_fin, tm_p).reshape(bsz, seq, d)
    y_sample = _rms_norm(xs, g_fin, ns_tok).reshape(db, ts, d)
    st = {n: jnp.stack(v) for n, v in outs.items()}
    to_heads = lambda a: a.reshape(depth, bsz, nheads_att, HEAD_DIM, seq).transpose(0, 1, 4, 2, 3)
    return (y_prompt, y_sample, to_heads(kt_all), to_heads(vt_all), st["sp"], st["hp"], st["ks"], st["vs"],
            st["ss"], st["hs"])
```
